```python
import math
import jax, jax.numpy as jnp
from jax import lax
import numpy as np

D_MODEL = 1024
BATCH = 2
SEQ = 16384
DEPTH = 4
DEC_BATCH = 4
DEC_SEQ = 8192
PAST_LEN = 128

GRID_W = 64
BRANCH_WIDTH = 512
N_BRANCH = 4
NA_HEADS = 8
NA_HEAD_DIM = BRANCH_WIDTH // NA_HEADS
NA_WR = 8
NA_WC = 16
SSM_HEADS = 8
SSM_HEAD_DIM = BRANCH_WIDTH // SSM_HEADS
SSM_GROUPS = 2
SSM_STATE = 64
SSM_CONV_W = 5
SSD_CHUNK = 128
XBC_W = BRANCH_WIDTH + 2 * SSM_GROUPS * SSM_STATE
ML_HEADS = 4
ML_HEAD_DIM = BRANCH_WIDTH // ML_HEADS
ML_CHUNK = 128
SGU_GROUPS = 4
SGU_CHUNK = 128
SGU_GROUP_DIM = BRANCH_WIDTH // SGU_GROUPS
D_FF = 2816
FFN_CONV_W = 3
NORM_EPS = 1e-6
NA_COLS = 3 * BRANCH_WIDTH
SSM_COLS = BRANCH_WIDTH + XBC_W + 2 * SSM_HEADS
ML_COLS = 4 * BRANCH_WIDTH + 4 * ML_HEADS
SGU_COLS = 2 * BRANCH_WIDTH
IN_COLS = NA_COLS + SSM_COLS + ML_COLS + SGU_COLS

kernel_name = 'hybrid_bidir_encoder_gated_branches'


def _rmsnorm(x, g):
    xf = x.astype(jnp.float32)
    xf = xf * lax.rsqrt(jnp.mean(xf * xf, axis=-1, keepdims=True) + NORM_EPS)
    return xf.astype(x.dtype) * g


def _flip(a):
    return jnp.flip(a, axis=1)


def _depthwise_conv(x, w, b):
    k, c = w.shape
    pad = k // 2
    y = lax.conv_general_dilated(x, w[:, None, :].astype(x.dtype), window_strides=(1,),
                                 padding=[(pad, pad)], dimension_numbers=('NWC', 'WIO', 'NWC'),
                                 feature_group_count=c)
    return y + b


def _neighbourhood_attention(q, k, v, rpb):
    bsz, seqlen, nh, dh = q.shape
    rows = seqlen // GRID_W
    wr = min(NA_WR, rows)
    qg = q.reshape(bsz, rows, GRID_W, nh, dh)
    kg = k.reshape(bsz, rows, GRID_W, nh, dh)
    vg = v.reshape(bsz, rows, GRID_W, nh, dh)
    cols = jnp.arange(GRID_W)
    col_start = jnp.clip(cols - NA_WC // 2, 0, GRID_W - NA_WC)
    col_idx = col_start[:, None] + jnp.arange(NA_WC)[None, :]
    dc = col_idx - cols[:, None] + (NA_WC - 1)
    rpb_f = rpb.astype(jnp.float32)

    def row_block(r):
        rs = jnp.clip(r - wr // 2, 0, rows - wr)
        kr = lax.dynamic_slice_in_dim(kg, rs, wr, axis=1)[:, :, col_idx]
        vr = lax.dynamic_slice_in_dim(vg, rs, wr, axis=1)[:, :, col_idx]
        qr = lax.dynamic_index_in_dim(qg, r, axis=1, keepdims=False)
        s = jnp.einsum('bqhd,biqjhd->bhqij', qr, kr).astype(jnp.float32)
        dr = rs + jnp.arange(wr) - r + (NA_WR - 1)
        bias = rpb_f[:, dr][:, :, dc]
        s = s + bias.transpose(0, 2, 1, 3)[None]
        p = jax.nn.softmax(s.reshape(bsz, nh, GRID_W, wr * NA_WC), axis=-1)
        p = p.reshape(s.shape).astype(v.dtype)
        return jnp.einsum('bhqij,biqjhd->bqhd', p, vr)

    out = lax.map(row_block, jnp.arange(rows))
    return out.transpose(1, 0, 2, 3, 4).reshape(bsz, seqlen, nh * dh)


def _ssd(x, dt, a, bm, cm):
    bsz, seqlen, nh, hp = x.shape
    ng, ns = bm.shape[2], bm.shape[3]
    hg = nh // ng
    nc = seqlen // SSD_CHUNK
    q = SSD_CHUNK
    xc = x.reshape(bsz, nc, q, ng, hg, hp)
    dtc = dt.reshape(bsz, nc, q, ng, hg)
    bc = bm.reshape(bsz, nc, q, ng, ns)
    cc = cm.reshape(bsz, nc, q, ng, ns)
    a_cs = jnp.cumsum(dtc * a.reshape(ng, hg), axis=2)
    tril = jnp.tril(jnp.ones((q, q), dtype=bool))[None, None, :, :, None, None]
    seg = a_cs[:, :, :, None] - a_cs[:, :, None, :]
    decay = jnp.exp(jnp.where(tril, seg, -jnp.inf))
    cb = jnp.einsum('bctgn,bcsgn->bctsg', cc, bc)
    wts = cb[..., None] * decay * dtc[:, :, None]
    y_diag = jnp.einsum('bctsgh,bcsghp->bctghp', wts, xc)
    last = a_cs[:, :, -1]
    xw = xc * (jnp.exp(last[:, :, None] - a_cs) * dtc)[..., None]
    states = jnp.einsum('bcsgn,bcsghp->bcghpn', bc, xw)

    def step(h, inp):
        st, dec = inp
        return h * dec[..., None, None] + st, h

    h0 = jnp.zeros((bsz, ng, hg, hp, ns), dtype=states.dtype)
    _, prev = lax.scan(step, h0, (jnp.moveaxis(states, 1, 0), jnp.moveaxis(jnp.exp(last), 1, 0)))
    prev = jnp.moveaxis(prev, 0, 1)
    y_off = jnp.einsum('bctgn,bcghpn->bctghp', cc, prev) * jnp.exp(a_cs)[..., None]
    return (y_diag + y_off).reshape(bsz, seqlen, nh, hp).astype(x.dtype)


def _mlstm(q, k, v, ig, lf):
    bsz, seqlen, nh, dh = q.shape
    nc = seqlen // ML_CHUNK
    def to_chunks(a):
        return a.astype(jnp.float32).reshape(bsz, nc, ML_CHUNK, nh, dh).transpose(1, 0, 3, 2, 4)
    def gate_chunks(a):
        return a.astype(jnp.float32).reshape(bsz, nc, ML_CHUNK, nh).transpose(1, 0, 3, 2)
    tril = jnp.tril(jnp.ones((ML_CHUNK, ML_CHUNK), dtype=bool))

    def step(carry, inp):
        c_st, n_st, m_st = carry
        qc, kc, vc, igc, lfc = inp
        fcs = jnp.cumsum(lfc, axis=-1)
        logd = jnp.where(tril, fcs[..., :, None] - fcs[..., None, :] + igc[..., None, :], -jnp.inf)
        m_inter = fcs + m_st[..., None]
        m_t = jnp.maximum(jnp.max(logd, axis=-1), m_inter)
        s = jnp.einsum('bhtd,bhsd->bhts', qc, kc) * jnp.exp(logd - m_t[..., None])
        sc = jnp.exp(m_inter - m_t)
        num = jnp.einsum('bhts,bhsd->bhtd', s, vc) + sc[..., None] * jnp.einsum('bhtk,bhkv->bhtv', qc, c_st)
        den = jnp.sum(s, axis=-1) + sc * jnp.einsum('bhtk,bhk->bht', qc, n_st)
        h = num / jnp.maximum(jnp.abs(den), jnp.exp(-m_t))[..., None]
        f_last = fcs[..., -1]
        logw = f_last[..., None] - fcs + igc
        m_new = jnp.maximum(f_last + m_st, jnp.max(logw, axis=-1))
        wk = jnp.exp(logw - m_new[..., None])
        dec = jnp.exp(f_last + m_st - m_new)
        c_new = dec[..., None, None] * c_st + jnp.einsum('bhsk,bhsv->bhkv', kc * wk[..., None], vc)
        n_new = dec[..., None] * n_st + jnp.einsum('bhs,bhsk->bhk', wk, kc)
        return (c_new, n_new, m_new), h

    init = (jnp.zeros((bsz, nh, dh, dh), jnp.float32), jnp.zeros((bsz, nh, dh), jnp.float32),
            jnp.zeros((bsz, nh), jnp.float32))
    _, h = lax.scan(step, init, (to_chunks(q), to_chunks(k), to_chunks(v), gate_chunks(ig), gate_chunks(lf)))
    return h.transpose(1, 0, 3, 2, 4).reshape(bsz, seqlen, nh, dh).astype(q.dtype)


def _token_mixer(h, w_in, na_rpb, ssm_conv_w, ssm_conv_b, ssm_dt_bias, ssm_a_log, ssm_d, ssm_norm_g,
                 ml_igate_b, ml_fgate_b, ml_norm_g, sgu_norm_g, sgu_w, sgu_b, w_branch, w_gate, b_gate, w_out):
    bsz, seqlen, _ = h.shape
    proj = h @ w_in
    o1 = NA_COLS
    o2 = o1 + SSM_COLS
    o3 = o2 + ML_COLS
    p_na, p_ssm, p_ml, p_sgu = proj[..., :o1], proj[..., o1:o2], proj[..., o2:o3], proj[..., o3:]

    qa, ka, va = [t.reshape(bsz, seqlen, NA_HEADS, NA_HEAD_DIM) for t in jnp.split(p_na, 3, axis=-1)]
    y_na = _neighbourhood_attention(qa * (NA_HEAD_DIM ** -0.5), ka, va, na_rpb)

    z = p_ssm[..., :BRANCH_WIDTH]
    xbc = jax.nn.silu(_depthwise_conv(p_ssm[..., BRANCH_WIDTH:BRANCH_WIDTH + XBC_W], ssm_conv_w, ssm_conv_b))
    dt_raw = p_ssm[..., BRANCH_WIDTH + XBC_W:]
    gn = SSM_GROUPS * SSM_STATE
    xs = xbc[..., :BRANCH_WIDTH].reshape(bsz, seqlen, SSM_HEADS, SSM_HEAD_DIM)
    bm = xbc[..., BRANCH_WIDTH:BRANCH_WIDTH + gn].reshape(bsz, seqlen, SSM_GROUPS, SSM_STATE)
    cm = xbc[..., BRANCH_WIDTH + gn:].reshape(bsz, seqlen, SSM_GROUPS, SSM_STATE)
    dt = jax.nn.softplus(dt_raw.reshape(bsz, seqlen, 2, SSM_HEADS).astype(jnp.float32)
                         + ssm_dt_bias.astype(jnp.float32))
    a = -jnp.exp(ssm_a_log.astype(jnp.float32))
    y_f = _ssd(xs, dt[:, :, 0], a[0], bm, cm)
    y_b = _flip(_ssd(_flip(xs), _flip(dt[:, :, 1]), a[1], _flip(bm), _flip(cm)))
    y = y_f + y_b + xs * ssm_d[:, None]
    y_ssm = _rmsnorm(y.reshape(bsz, seqlen, BRANCH_WIDTH) * jax.nn.silu(z), ssm_norm_g)

    qm, km, vm, om = [p_ml[..., i * BRANCH_WIDTH:(i + 1) * BRANCH_WIDTH] for i in range(4)]
    qm = qm.reshape(bsz, seqlen, ML_HEADS, ML_HEAD_DIM)
    km = km.reshape(bsz, seqlen, ML_HEADS, ML_HEAD_DIM) * (ML_HEAD_DIM ** -0.5)
    vm = vm.reshape(bsz, seqlen, ML_HEADS, ML_HEAD_DIM)
    gates = p_ml[..., 4 * BRANCH_WIDTH:].astype(jnp.float32).reshape(bsz, seqlen, 2, 2, ML_HEADS)
    ig = gates[:, :, 0] + ml_igate_b.astype(jnp.float32)
    lf = jax.nn.log_sigmoid(gates[:, :, 1] + ml_fgate_b.astype(jnp.float32))
    h_f = _mlstm(qm, km, vm, ig[:, :, 0], lf[:, :, 0])
    h_b = _flip(_mlstm(_flip(qm), _flip(km), _flip(vm), _flip(ig[:, :, 1]), _flip(lf[:, :, 1])))
    hm = _rmsnorm(h_f + h_b, ml_norm_g.reshape(ML_HEADS, ML_HEAD_DIM))
    y_ml = hm.reshape(bsz, seqlen, BRANCH_WIDTH) * jax.nn.sigmoid(om)

    u = jax.nn.gelu(p_sgu[..., :BRANCH_WIDTH])
    vg = _rmsnorm(jax.nn.gelu(p_sgu[..., BRANCH_WIDTH:]), sgu_norm_g)
    vg = vg.reshape(bsz, seqlen // SGU_CHUNK, SGU_CHUNK, SGU_GROUPS, SGU_GROUP_DIM)
    sv = jnp.einsum('gts,bcsgd->bctgd', sgu_w, vg) + sgu_b.T[:, :, None]
    y_sgu = u * sv.reshape(bsz, seqlen, BRANCH_WIDTH)

    g = jax.nn.sigmoid(h @ w_gate + b_gate.reshape(-1)).reshape(bsz, seqlen, N_BRANCH, D_MODEL)
    branches = (y_na, y_ssm, y_ml, y_sgu)
    merged = g[:, :, 0] * (branches[0] @ w_branch[0])
    for i in range(1, N_BRANCH):
        merged = merged + g[:, :, i] * (branches[i] @ w_branch[i])
    return merged @ w_out


def _conv_ffn(h, w_up, conv_w, conv_b, w_down):
    up = _depthwise_conv(h @ w_up, conv_w, conv_b)
    a, b = jnp.split(up, 2, axis=-1)
    return (jax.nn.gelu(a) * b) @ w_down


def _encode(x, norm_mix_g, w_in, na_rpb, ssm_conv_w, ssm_conv_b, ssm_dt_bias, ssm_a_log, ssm_d, ssm_norm_g,
            ml_igate_b, ml_fgate_b, ml_norm_g, sgu_norm_g, sgu_w, sgu_b, w_branch, w_gate, b_gate, w_out,
            norm_ffn_g, ffn_w_up, ffn_conv_w, ffn_conv_b, ffn_w_down, norm_final_g):
    for l in range(DEPTH):
        x = x + _token_mixer(_rmsnorm(x, norm_mix_g[l]), w_in[l], na_rpb[l], ssm_conv_w[l], ssm_conv_b[l],
                             ssm_dt_bias[l], ssm_a_log[l], ssm_d[l], ssm_norm_g[l], ml_igate_b[l],
                             ml_fgate_b[l], ml_norm_g[l], sgu_norm_g[l], sgu_w[l], sgu_b[l], w_branch[l],
                             w_gate[l], b_gate[l], w_out[l])
        x = x + _conv_ffn(_rmsnorm(x, norm_ffn_g[l]), ffn_w_up[l], ffn_conv_w[l], ffn_conv_b[l], ffn_w_down[l])
    return _rmsnorm(x, norm_final_g)


def setup_inputs(seed: int = 0) -> dict:
    key = jax.random.key(seed)
    ks = jax.random.split(key, 32)
    f32 = jnp.float32
    def nrm(k, shape, scale):
        return jax.random.normal(k, shape, f32) * scale
    dt0 = jnp.exp(jax.random.uniform(ks[7], (DEPTH, 2, SSM_HEADS), f32, math.log(1e-3), math.log(1e-1)))
    return {
        'x_prompt': nrm(ks[0], (BATCH, SEQ, D_MODEL), 1.0),
        'x_sample': nrm(ks[1], (DEC_BATCH, DEC_SEQ, D_MODEL), 1.0),
        'norm_mix_g': 1.0 + nrm(ks[2], (DEPTH, D_MODEL), 0.02),
        'w_in': nrm(ks[3], (DEPTH, D_MODEL, IN_COLS), D_MODEL ** -0.5),
        'na_rpb': nrm(ks[4], (DEPTH, NA_HEADS, 2 * NA_WR - 1, 2 * NA_WC - 1), 0.1),
        'ssm_conv_w': nrm(ks[5], (DEPTH, SSM_CONV_W, XBC_W), SSM_CONV_W ** -0.5),
        'ssm_conv_b': nrm(ks[6], (DEPTH, XBC_W), 0.01),
        'ssm_dt_bias': dt0 + jnp.log(-jnp.expm1(-dt0)),
        'ssm_a_log': jnp.log(jax.random.uniform(ks[8], (DEPTH, 2, SSM_HEADS), f32, 1.0, 16.0)),
        'ssm_d': 1.0 + nrm(ks[9], (DEPTH, SSM_HEADS), 0.1),
        'ssm_norm_g': 1.0 + nrm(ks[10], (DEPTH, BRANCH_WIDTH), 0.02),
        'ml_igate_b': nrm(ks[11], (DEPTH, 2, ML_HEADS), 0.1),
        'ml_fgate_b': jnp.linspace(3.0, 6.0, ML_HEADS, dtype=f32) + nrm(ks[12], (DEPTH, 2, ML_HEADS), 0.1),
        'ml_norm_g': 1.0 + nrm(ks[13], (DEPTH, BRANCH_WIDTH), 0.02),
        'sgu_norm_g': 1.0 + nrm(ks[14], (DEPTH, BRANCH_WIDTH), 0.02),
        'sgu_w': nrm(ks[15], (DEPTH, SGU_GROUPS, SGU_CHUNK, SGU_CHUNK), SGU_CHUNK ** -0.5),
        'sgu_b': 1.0 + nrm(ks[16], (DEPTH, SGU_GROUPS, SGU_CHUNK), 0.02),
        'w_branch': nrm(ks[17], (DEPTH, N_BRANCH, BRANCH_WIDTH, D_MODEL), BRANCH_WIDTH ** -0.5),
        'w_gate': nrm(ks[18], (DEPTH, D_MODEL, N_BRANCH * D_MODEL), D_MODEL ** -0.5),
        'b_gate': nrm(ks[19], (DEPTH, N_BRANCH, D_MODEL), 0.01),
        'w_out': nrm(ks[20], (DEPTH, D_MODEL, D_MODEL), D_MODEL ** -0.5),
        'norm_ffn_g': 1.0 + nrm(ks[21], (DEPTH, D_MODEL), 0.02),
        'ffn_w_up': nrm(ks[22], (DEPTH, D_MODEL, 2 * D_FF), D_MODEL ** -0.5),
        'ffn_conv_w': nrm(ks[23], (DEPTH, FFN_CONV_W, 2 * D_FF), FFN_CONV_W ** -0.5),
        'ffn_conv_b': nrm(ks[24], (DEPTH, 2 * D_FF), 0.01),
        'ffn_w_down': nrm(ks[25], (DEPTH, D_FF, D_MODEL), D_FF ** -0.5),
        'norm_final_g': 1.0 + nrm(ks[26], (D_MODEL,), 0.02),
    }


def reference(x_prompt, x_sample, norm_mix_g, w_in, na_rpb, ssm_conv_w, ssm_conv_b, ssm_dt_bias, ssm_a_log,
              ssm_d, ssm_norm_g, ml_igate_b, ml_fgate_b, ml_norm_g, sgu_norm_g, sgu_w, sgu_b, w_branch,
              w_gate, b_gate, w_out, norm_ffn_g, ffn_w_up, ffn_conv_w, ffn_conv_b, ffn_w_down, norm_final_g):
    weights = (norm_mix_g, w_in, na_rpb, ssm_conv_w, ssm_conv_b, ssm_dt_bias, ssm_a_log, ssm_d, ssm_norm_g,
               ml_igate_b, ml_fgate_b, ml_norm_g, sgu_norm_g, sgu_w, sgu_b, w_branch, w_gate, b_gate, w_out,
               norm_ffn_g, ffn_w_up, ffn_conv_w, ffn_conv_b, ffn_w_down, norm_final_g)
    y_prompt = _encode(x_prompt, *weights)
    y_sample = _encode(x_sample, *weights)
    return (y_prompt, y_sample)
```

```python
import functools
import math

import jax
import jax.numpy as jnp
import numpy as np
from jax import lax
from jax.experimental import pallas as pl
from jax.experimental.pallas import tpu as pltpu

F32 = jnp.float32
BF16 = jnp.bfloat16

D_MODEL = 1024
GRID_W = 64
BRANCH_WIDTH = 512
N_BRANCH = 4
NA_HEADS = 8
NA_HEAD_DIM = BRANCH_WIDTH // NA_HEADS
NA_WR = 8
NA_WC = 16
SSM_HEADS = 8
SSM_HEAD_DIM = BRANCH_WIDTH // SSM_HEADS
SSM_GROUPS = 2
SSM_STATE = 64
SSM_CONV_W = 5
CHUNK = 128
XBC_W = BRANCH_WIDTH + 2 * SSM_GROUPS * SSM_STATE
ML_HEADS = 4
ML_HEAD_DIM = BRANCH_WIDTH // ML_HEADS
SGU_GROUPS = 4
SGU_GROUP_DIM = BRANCH_WIDTH // SGU_GROUPS
D_FF = 2816
FFN_CONV_W = 3
NORM_EPS = 1e-6
NA_COLS = 3 * BRANCH_WIDTH
SSM_COLS = BRANCH_WIDTH + XBC_W + 2 * SSM_HEADS
ML_COLS = 4 * BRANCH_WIDTH + 4 * ML_HEADS

LANES = 128
SUBLANES = 8
BF16_ROWS = 16
VMEM_LIMIT = 56 * 1024 * 1024

TOKEN_TILE = 512
NA_BLOCK_ROWS = TOKEN_TILE // GRID_W
FFN_COL_CHUNK = 256
MASK_VALUE = -1e30

PROJ_BLOCKS = dict(na_q=0, na_k=1, na_v=2, ssm_z=3, ssm_x=4, ml_q=5, ml_k=6, ml_v=7, ml_o=8,
                   sgu_u=9, sgu_v=10)
PROJ_MAIN_W = 11 * BRANCH_WIDTH + 2 * SSM_GROUPS * SSM_STATE
BC_W = 2 * SSM_GROUPS * SSM_STATE
BC_BLOCK = (11 * BRANCH_WIDTH) // BC_W
SMALL_DT = 0
SMALL_IG = 2 * SSM_HEADS
SMALL_FG = SMALL_IG + 2 * ML_HEADS


def _cparams(sem):
    return pltpu.CompilerParams(dimension_semantics=sem, vmem_limit_bytes=VMEM_LIMIT)


def _resident(shape):
    nd = len(shape)
    return pl.BlockSpec(shape, lambda *_: (0,) * nd, pipeline_mode=pl.Buffered(1))


def _seg_pos(pos, geom):
    n_prompt, len_prompt, len_sample = geom
    in_prompt = pos < n_prompt
    local = jnp.where(in_prompt, pos % len_prompt, (pos - n_prompt) % len_sample)
    return local, jnp.where(in_prompt, len_prompt, len_sample)


def _rms(x):
    return x * lax.rsqrt(jnp.mean(x * x, axis=-1, keepdims=True) + NORM_EPS)


def _sigmoid(x):
    return 1.0 / (1.0 + jnp.exp(-x))


def _softplus(x):
    return jnp.maximum(x, 0.0) + jnp.log(1.0 + jnp.exp(-jnp.abs(x)))


def _gelu_tanh(x):
    return 0.5 * x * (1.0 + jnp.tanh(math.sqrt(2.0 / math.pi) * (x + 0.044715 * (x * x * x))))


def _dot(a, b):
    return jnp.dot(a, b, preferred_element_type=F32)


def _dot_nt(a, b):
    return lax.dot_general(a, b, (((1,), (1,)), ((), ())), preferred_element_type=F32)


def _cumsum_rows(tri, x):
    return jnp.dot(tri, x, preferred_element_type=F32, precision=lax.Precision.HIGHEST)


def _chunk_masks(reverse):
    t = lax.broadcasted_iota(jnp.int32, (CHUNK, CHUNK), 0)
    s = lax.broadcasted_iota(jnp.int32, (CHUNK, CHUNK), 1)
    mask = (s >= t) if reverse else (s <= t)
    return mask, jnp.where(mask, 1.0, 0.0).astype(F32)


def _norm_matmul_body(*refs, chunks, has_small):
    if has_small:
        x_ref, g_ref, w_ref, ws_ref, o_ref, os_ref = refs
    else:
        x_ref, g_ref, w_ref, o_ref = refs
    h = (_rms(x_ref[...]) * g_ref[...]).astype(BF16)
    for c0, cw in chunks:
        o_ref[:, c0:c0 + cw] = _dot(h, w_ref[:, c0:c0 + cw]).astype(o_ref.dtype)
    if has_small:
        os_ref[...] = _dot(h, ws_ref[...])


def _col_chunks(width, step=512):
    out, c = [], 0
    while c < width:
        out.append((c, min(step, width - c)))
        c += step
    return tuple(out)


def _norm_matmul(x, g, w, w_small=None, *, name):
    n, d = x.shape
    cols = w.shape[1]
    has_small = w_small is not None
    in_specs = [pl.BlockSpec((TOKEN_TILE, d), lambda i: (i, 0)), _resident((1, d)), _resident(w.shape)]
    out_shape = [jax.ShapeDtypeStruct((n, cols), BF16)]
    out_specs = [pl.BlockSpec((TOKEN_TILE, cols), lambda i: (i, 0))]
    args = [x, g.reshape(1, d), w]
    if has_small:
        in_specs.append(_resident(w_small.shape))
        out_shape.append(jax.ShapeDtypeStruct((n, w_small.shape[1]), F32))
        out_specs.append(pl.BlockSpec((TOKEN_TILE, w_small.shape[1]), lambda i: (i, 0)))
        args.append(w_small)
    res = pl.pallas_call(
        functools.partial(_norm_matmul_body, chunks=_col_chunks(cols), has_small=has_small),
        out_shape=out_shape, grid=(n // TOKEN_TILE,), in_specs=in_specs, out_specs=out_specs,
        compiler_params=_cparams(("parallel",)), name=name)(*args)
    return res if has_small else res[0]


def _na_bias_table(rpb):
    cols = np.arange(GRID_W)
    col_start = np.clip(cols - NA_WC // 2, 0, GRID_W - NA_WC)
    valid = (cols[None, :] >= col_start[:, None]) & (cols[None, :] < col_start[:, None] + NA_WC)
    dc = np.clip(cols[None, :] - cols[:, None] + (NA_WC - 1), 0, 2 * NA_WC - 2)
    dr = np.arange(NA_WR)[None, :] - np.arange(NA_WR)[:, None] + (NA_WR - 1)
    t = rpb.astype(F32)[:, dr][:, :, :, dc]
    t = jnp.where(jnp.asarray(valid)[None, None, None], t, MASK_VALUE)
    return t.transpose(1, 0, 3, 2, 4).reshape(NA_WR, NA_HEADS, GRID_W, NA_WR * GRID_W)


def _na_body(q_ref, kp_ref, kc_ref, kn_ref, vp_ref, vc_ref, vn_ref, bias_ref, o_ref, kbuf, vbuf, *, geom):
    local, seg = _seg_pos(pl.program_id(0) * TOKEN_TILE, geom)
    first = local == 0
    last = local + TOKEN_TILE == seg
    t = TOKEN_TILE
    kbuf[0:t, :] = kp_ref[...]
    kbuf[t:2 * t, :] = kc_ref[...]
    kbuf[2 * t:3 * t, :] = kn_ref[...]
    vbuf[0:t, :] = vp_ref[...]
    vbuf[t:2 * t, :] = vc_ref[...]
    vbuf[2 * t:3 * t, :] = vn_ref[...]
    lo = jnp.where(first, 0, -NA_WR)
    hi = jnp.where(last, 0, NA_WR)
    win = NA_WR * GRID_W

    def row_body(j, carry):
        start = NA_BLOCK_ROWS + jnp.clip(j - NA_WR // 2, lo, hi)
        off = NA_BLOCK_ROWS + j - start
        r0 = pl.multiple_of(start * GRID_W, GRID_W)
        q0 = pl.multiple_of(j * GRID_W, GRID_W)
        qrow = q_ref[pl.ds(q0, GRID_W), :]
        krows = kbuf[pl.ds(r0, win), :]
        vrows = vbuf[pl.ds(r0, win), :]
        outs = []
        for h in range(NA_HEADS):
            hs = slice(h * NA_HEAD_DIM, (h + 1) * NA_HEAD_DIM)
            qh = qrow[:, hs] * jnp.asarray(NA_HEAD_DIM ** -0.5, BF16)
            s = _dot_nt(qh, krows[:, hs]) + bias_ref[off, h]
            p = jnp.exp(s - jnp.max(s, axis=-1, keepdims=True))
            l = jnp.sum(p, axis=-1, keepdims=True)
            outs.append(_dot(p.astype(BF16), vrows[:, hs]) / l)
        o_ref[pl.ds(q0, GRID_W), :] = jnp.concatenate(outs, axis=1).astype(o_ref.dtype)
        return carry

    lax.fori_loop(0, NA_BLOCK_ROWS, row_body, 0)


def _neighbourhood_attention(proj, bias_table, geom):
    n = proj.shape[0]
    nb = n // TOKEN_TILE
    w = BRANCH_WIDTH

    def spec(col, shift):
        return pl.BlockSpec((TOKEN_TILE, w), lambda i: (jnp.clip(i + shift, 0, nb - 1), col))

    kq, kk, kv = PROJ_BLOCKS["na_q"], PROJ_BLOCKS["na_k"], PROJ_BLOCKS["na_v"]
    return pl.pallas_call(
        functools.partial(_na_body, geom=geom),
        out_shape=jax.ShapeDtypeStruct((n, w), BF16), grid=(nb,),
        in_specs=[spec(kq, 0), spec(kk, -1), spec(kk, 0), spec(kk, 1), spec(kv, -1), spec(kv, 0), spec(kv, 1),
                  _resident(bias_table.shape)],
        out_specs=pl.BlockSpec((TOKEN_TILE, w), lambda i: (i, 0)),
        scratch_shapes=[pltpu.VMEM((3 * TOKEN_TILE, w), BF16), pltpu.VMEM((3 * TOKEN_TILE, w), BF16)],
        compiler_params=_cparams(("parallel",)), name="na_attention")(
            proj, proj, proj, proj, proj, proj, proj, bias_table)


def _dwconv_tile(scr_ref, cur, prev, nxt, w, bias, first, last):
    t = cur.shape[0]
    kw = w.shape[0]
    pad = kw // 2
    scr_ref[0:SUBLANES, :] = jnp.where(first, 0.0, prev[BF16_ROWS - SUBLANES:, :])
    scr_ref[SUBLANES:SUBLANES + t, :] = cur
    scr_ref[SUBLANES + t:2 * SUBLANES + t, :] = jnp.where(last, 0.0, nxt[:SUBLANES, :])
    acc = bias
    for k in range(kw):
        r = SUBLANES + k - pad
        acc = acc + w[k:k + 1, :] * scr_ref[r:r + t, :]
    return acc


def _halo_specs(n, width, col):
    per = TOKEN_TILE // BF16_ROWS
    nh = n // BF16_ROWS
    return [pl.BlockSpec((TOKEN_TILE, width), lambda i: (i, col)),
            pl.BlockSpec((BF16_ROWS, width), lambda i: (jnp.maximum(i * per - 1, 0), col)),
            pl.BlockSpec((BF16_ROWS, width), lambda i: (jnp.minimum((i + 1) * per, nh - 1), col))]


def _tile_ends(geom):
    local, seg = _seg_pos(pl.program_id(0) * TOKEN_TILE, geom)
    return local == 0, local + TOKEN_TILE == seg


def _conv_silu_body(c_ref, p_ref, n_ref, w_ref, b_ref, o_ref, scr_ref, *, geom):
    first, last = _tile_ends(geom)
    y = _dwconv_tile(scr_ref, c_ref[...].astype(F32), p_ref[...].astype(F32), n_ref[...].astype(F32),
                     w_ref[...], b_ref[...], first, last)
    o_ref[...] = (y * _sigmoid(y)).astype(o_ref.dtype)


def _conv_silu(proj, col, width, w, b, geom, *, name):
    n = proj.shape[0]
    return pl.pallas_call(
        functools.partial(_conv_silu_body, geom=geom),
        out_shape=jax.ShapeDtypeStruct((n, width), BF16), grid=(n // TOKEN_TILE,),
        in_specs=_halo_specs(n, width, col) + [_resident(w.shape), _resident((1, width))],
        out_specs=pl.BlockSpec((TOKEN_TILE, width), lambda i: (i, 0)),
        scratch_shapes=[pltpu.VMEM((TOKEN_TILE + 2 * SUBLANES, width), F32)],
        compiler_params=_cparams(("parallel",)), name=name)(proj, proj, proj, w, b.reshape(1, width))


def _ssd_body(*refs, geom, reverse, n_tiles):
    if reverse:
        x_ref, bc_ref, sm_ref, dtb_ref, a_ref, yf_ref, z_ref, d_ref, g_ref, o_ref, h_ref = refs
    else:
        x_ref, bc_ref, sm_ref, dtb_ref, a_ref, o_ref, h_ref = refs
    step = pl.program_id(0)
    tile = (n_tiles - 1 - step) if reverse else step
    mask, tri = _chunk_masks(reverse)
    dir_off = SMALL_DT + (SSM_HEADS if reverse else 0)
    hg = SSM_HEADS // SSM_GROUPS
    gw = hg * SSM_HEAD_DIM
    subs = TOKEN_TILE // CHUNK

    def sub_body(it, carry):
        sub = (subs - 1 - it) if reverse else it
        local, seg = _seg_pos(tile * TOKEN_TILE + sub * CHUNK, geom)
        start = (local + CHUNK == seg) if reverse else (local == 0)

        @pl.when(start)
        def _():
            h_ref[...] = jnp.zeros_like(h_ref)

        rows = pl.ds(pl.multiple_of(sub * CHUNK, CHUNK), CHUNK)
        x = x_ref[rows, :]
        bc = bc_ref[rows, :].astype(F32)
        dt = _softplus(sm_ref[rows, :] + dtb_ref[...])
        cs = _cumsum_rows(tri, dt * a_ref[...])
        cs_row = cs.T
        dt_row = dt.T
        tot = cs[0:1, :] if reverse else cs[CHUNK - 1:CHUNK, :]
        bc_t = bc.T
        ygs = []
        for g in range(SSM_GROUPS):
            b_g = bc[:, g * SSM_STATE:(g + 1) * SSM_STATE].astype(BF16)
            c_lo = SSM_GROUPS * SSM_STATE + g * SSM_STATE
            c_g = bc[:, c_lo:c_lo + SSM_STATE].astype(BF16)
            b_gt = bc_t[g * SSM_STATE:(g + 1) * SSM_STATE, :].astype(BF16)
            cb = _dot_nt(c_g, b_g)
            xg = x[:, g * gw:(g + 1) * gw].astype(F32)
            h_prev = h_ref[g]
            y_off = _dot(c_g, h_prev.astype(BF16))
            ys, xws, decs = [], [], []
            for j in range(hg):
                ch = dir_off + g * hg + j
                a_c = cs[:, ch:ch + 1]
                decay = jnp.exp(jnp.where(mask, a_c - cs_row[ch:ch + 1, :], -jnp.inf))
                wts = (cb * decay * dt_row[ch:ch + 1, :]).astype(BF16)
                hs = slice(j * SSM_HEAD_DIM, (j + 1) * SSM_HEAD_DIM)
                xh = x[:, g * gw + j * SSM_HEAD_DIM:g * gw + (j + 1) * SSM_HEAD_DIM]
                ys.append(_dot(wts, xh) + y_off[:, hs] * jnp.exp(a_c))
                tot_h = tot[:, ch:ch + 1]
                xws.append(xg[:, hs] * (jnp.exp(tot_h - a_c) * dt[:, ch:ch + 1]))
                decs.append(jnp.broadcast_to(jnp.exp(tot_h), (1, SSM_HEAD_DIM)))
            xw = jnp.concatenate(xws, axis=1).astype(BF16)
            h_ref[g] = h_prev * jnp.concatenate(decs, axis=1) + _dot(b_gt, xw)
            ygs.append(jnp.concatenate(ys, axis=1))
        y = jnp.concatenate(ygs, axis=1)
        if reverse:
            y = yf_ref[rows, :] + y + x.astype(F32) * d_ref[...]
            z = z_ref[rows, :].astype(F32)
            o_ref[rows, :] = (_rms(y * (z * _sigmoid(z))) * g_ref[...]).astype(o_ref.dtype)
        else:
            o_ref[rows, :] = y
        return carry

    lax.fori_loop(0, subs, sub_body, 0)


def _ssd(xc, bcc, small, dtb_row, a_row, geom, *, reverse, y_fwd=None, proj=None, d_row=None, g_row=None):
    n = xc.shape[0]
    nt = n // TOKEN_TILE
    w = BRANCH_WIDTH

    def tile_spec(width, col=0):
        if reverse:
            return pl.BlockSpec((TOKEN_TILE, width), lambda i: (nt - 1 - i, col))
        return pl.BlockSpec((TOKEN_TILE, width), lambda i: (i, col))

    in_specs = [tile_spec(w), tile_spec(BC_W), tile_spec(LANES), _resident((1, LANES)), _resident((1, LANES))]
    args = [xc, bcc, small, dtb_row, a_row]
    if reverse:
        in_specs += [tile_spec(w), tile_spec(w, PROJ_BLOCKS["ssm_z"]), _resident((1, w)), _resident((1, w))]
        args += [y_fwd, proj, d_row, g_row]
    return pl.pallas_call(
        functools.partial(_ssd_body, geom=geom, reverse=reverse, n_tiles=nt),
        out_shape=jax.ShapeDtypeStruct((n, w), BF16 if reverse else F32), grid=(nt,),
        in_specs=in_specs, out_specs=tile_spec(w),
        scratch_shapes=[pltpu.VMEM((SSM_GROUPS, SSM_STATE, w // SSM_GROUPS), F32)],
        compiler_params=_cparams(("arbitrary",)), name="ssd_bwd" if reverse else "ssd_fwd")(*args)


def _mlstm_body(*refs, geom, reverse, n_tiles):
    if reverse:
        q_ref, k_ref, v_ref, sm_ref, gb_ref, hf_ref, og_ref, g_ref, o_ref, c_ref, n_ref, m_ref = refs
    else:
        q_ref, k_ref, v_ref, sm_ref, gb_ref, o_ref, c_ref, n_ref, m_ref = refs
    step = pl.program_id(0)
    tile = (n_tiles - 1 - step) if reverse else step
    mask, tri = _chunk_masks(reverse)
    ig_off = SMALL_IG + (ML_HEADS if reverse else 0)
    fg_off = SMALL_FG + (ML_HEADS if reverse else 0)
    subs = TOKEN_TILE // CHUNK
    k_scale = ML_HEAD_DIM ** -0.5

    def sub_body(it, carry):
        sub = (subs - 1 - it) if reverse else it
        local, seg = _seg_pos(tile * TOKEN_TILE + sub * CHUNK, geom)
        start = (local + CHUNK == seg) if reverse else (local == 0)

        @pl.when(start)
        def _():
            c_ref[...] = jnp.zeros_like(c_ref)
            n_ref[...] = jnp.zeros_like(n_ref)
            m_ref[...] = jnp.zeros_like(m_ref)

        rows = pl.ds(pl.multiple_of(sub * CHUNK, CHUNK), CHUNK)
        q = q_ref[rows, :]
        k = k_ref[rows, :]
        v = v_ref[rows, :]
        gates = sm_ref[rows, :] + gb_ref[...]
        fcs = _cumsum_rows(tri, -_softplus(-gates))
        fcs_row = fcs.T
        gates_row = gates.T
        ftot = fcs[0:1, :] if reverse else fcs[CHUNK - 1:CHUNK, :]
        hs_out = []
        for h in range(ML_HEADS):
            ci, cf = ig_off + h, fg_off + h
            hsl = slice(h * ML_HEAD_DIM, (h + 1) * ML_HEAD_DIM)
            fc = fcs[:, cf:cf + 1]
            ig_c = gates[:, ci:ci + 1]
            m_st = m_ref[h:h + 1, 0:1]
            logd = jnp.where(mask, fc - fcs_row[cf:cf + 1, :] + gates_row[ci:ci + 1, :], -jnp.inf)
            m_inter = fc + m_st
            m_t = jnp.maximum(jnp.max(logd, axis=-1, keepdims=True), m_inter)
            qh = q[:, hsl]
            kh = k[:, hsl].astype(F32) * k_scale
            vh = v[:, hsl]
            s = _dot_nt(qh, kh.astype(BF16)) * jnp.exp(logd - m_t)
            sc = jnp.exp(m_inter - m_t)
            c_st = c_ref[h]
            n_st = n_ref[h:h + 1, :]
            num = _dot(s.astype(BF16), vh) + sc * _dot(qh, c_st.astype(BF16))
            den = (jnp.sum(s, axis=-1, keepdims=True)
                   + sc * jnp.sum(qh.astype(F32) * n_st, axis=-1, keepdims=True))
            hs_out.append(num / jnp.maximum(jnp.abs(den), jnp.exp(-m_t)))
            f_last = ftot[:, cf:cf + 1]
            logw = f_last - fc + ig_c
            m_new = jnp.maximum(f_last + m_st, jnp.max(logw, axis=0, keepdims=True))
            kw = kh * jnp.exp(logw - m_new)
            dec = jnp.exp(f_last + m_st - m_new)
            c_ref[h] = dec * c_st + _dot(kw.T.astype(BF16), vh)
            n_ref[h:h + 1, :] = dec * n_st + jnp.sum(kw, axis=0, keepdims=True)
            m_ref[h:h + 1, :] = jnp.broadcast_to(m_new, (1, LANES))
        if reverse:
            hf = hf_ref[rows, :]
            og = og_ref[rows, :].astype(F32)
            outs = []
            for h in range(ML_HEADS):
                hsl = slice(h * ML_HEAD_DIM, (h + 1) * ML_HEAD_DIM)
                outs.append(_rms(hf[:, hsl] + hs_out[h]) * g_ref[:, hsl] * _sigmoid(og[:, hsl]))
            o_ref[rows, :] = jnp.concatenate(outs, axis=1).astype(o_ref.dtype)
        else:
            o_ref[rows, :] = jnp.concatenate(hs_out, axis=1)
        return carry

    lax.fori_loop(0, subs, sub_body, 0)


def _mlstm(proj, small, gate_bias_row, geom, *, reverse, h_fwd=None, g_row=None):
    n = proj.shape[0]
    nt = n // TOKEN_TILE
    w = BRANCH_WIDTH

    def tile_spec(width, col=0):
        if reverse:
            return pl.BlockSpec((TOKEN_TILE, width), lambda i: (nt - 1 - i, col))
        return pl.BlockSpec((TOKEN_TILE, width), lambda i: (i, col))

    in_specs = [tile_spec(w, PROJ_BLOCKS["ml_q"]), tile_spec(w, PROJ_BLOCKS["ml_k"]),
                tile_spec(w, PROJ_BLOCKS["ml_v"]), tile_spec(LANES), _resident((1, LANES))]
    args = [proj, proj, proj, small, gate_bias_row]
    if reverse:
        in_specs += [tile_spec(w), tile_spec(w, PROJ_BLOCKS["ml_o"]), _resident((1, w))]
        args += [h_fwd, proj, g_row]
    return pl.pallas_call(
        functools.partial(_mlstm_body, geom=geom, reverse=reverse, n_tiles=nt),
        out_shape=jax.ShapeDtypeStruct((n, w), BF16 if reverse else F32), grid=(nt,),
        in_specs=in_specs, out_specs=tile_spec(w),
        scratch_shapes=[pltpu.VMEM((ML_HEADS, ML_HEAD_DIM, ML_HEAD_DIM), F32),
                        pltpu.VMEM((SUBLANES, ML_HEAD_DIM), F32), pltpu.VMEM((SUBLANES, LANES), F32)],
        compiler_params=_cparams(("arbitrary",)), name="mlstm_bwd" if reverse else "mlstm_fwd")(*args)


def _sgu_body(u_ref, v_ref, g_ref, w_ref, b_ref, o_ref):
    u = _gelu_tanh(u_ref[...].astype(F32))
    vg = (_rms(_gelu_tanh(v_ref[...].astype(F32))) * g_ref[...]).astype(BF16)
    for c in range(TOKEN_TILE // CHUNK):
        rows = slice(c * CHUNK, (c + 1) * CHUNK)
        outs = []
        for g in range(SGU_GROUPS):
            cols = slice(g * SGU_GROUP_DIM, (g + 1) * SGU_GROUP_DIM)
            outs.append(_dot(w_ref[g], vg[rows, cols]) + b_ref[:, g:g + 1])
        o_ref[rows, :] = (u[rows, :] * jnp.concatenate(outs, axis=1)).astype(o_ref.dtype)


def _sgu(proj, g_row, w, b_cols):
    n = proj.shape[0]
    wd = BRANCH_WIDTH
    return pl.pallas_call(
        _sgu_body, out_shape=jax.ShapeDtypeStruct((n, wd), BF16), grid=(n // TOKEN_TILE,),
        in_specs=[pl.BlockSpec((TOKEN_TILE, wd), lambda i: (i, PROJ_BLOCKS["sgu_u"])),
                  pl.BlockSpec((TOKEN_TILE, wd), lambda i: (i, PROJ_BLOCKS["sgu_v"])),
                  _resident((1, wd)), _resident(w.shape), _resident(b_cols.shape)],
        out_specs=pl.BlockSpec((TOKEN_TILE, wd), lambda i: (i, 0)),
        compiler_params=_cparams(("parallel",)), name="sgu")(proj, proj, g_row, w, b_cols)


def _merge_body(x_ref, gm_ref, wg_ref, bg_ref, y0_ref, y1_ref, y2_ref, y3_ref, wb_ref, wo_ref, o_ref):
    x = x_ref[...]
    h = (_rms(x) * gm_ref[...]).astype(BF16)
    merged = None
    for i, y_ref in enumerate((y0_ref, y1_ref, y2_ref, y3_ref)):
        cols = slice(i * D_MODEL, (i + 1) * D_MODEL)
        term = _sigmoid(_dot(h, wg_ref[:, cols]) + bg_ref[:, cols]) * _dot(y_ref[...], wb_ref[i])
        merged = term if merged is None else merged + term
    o_ref[...] = x + _dot(merged.astype(BF16), wo_ref[...])


def _merge(x, gm_row, wg, bg_row, branches, wb, wo):
    n, d = x.shape
    br_spec = pl.BlockSpec((TOKEN_TILE, BRANCH_WIDTH), lambda i: (i, 0))
    x_spec = pl.BlockSpec((TOKEN_TILE, d), lambda i: (i, 0))
    return pl.pallas_call(
        _merge_body, out_shape=jax.ShapeDtypeStruct((n, d), F32), grid=(n // TOKEN_TILE,),
        in_specs=[x_spec, _resident((1, d)), _resident(wg.shape), _resident(bg_row.shape),
                  br_spec, br_spec, br_spec, br_spec, _resident(wb.shape), _resident(wo.shape)],
        out_specs=x_spec, compiler_params=_cparams(("parallel",)), name="merge")(
            x, gm_row, wg, bg_row, *branches, wb, wo)


def _ffn_tail_body(x_ref, a_ref, ap_ref, an_ref, b_ref, bp_ref, bn_ref, cwa_ref, cba_ref, cwb_ref, cbb_ref,
                   wd_ref, gf_ref, o_ref, acc_ref, scr_ref, *, geom, final):
    first, last = _tile_ends(geom)
    acc_ref[...] = x_ref[...]
    for c in range(D_FF // FFN_COL_CHUNK):
        cols = slice(c * FFN_COL_CHUNK, (c + 1) * FFN_COL_CHUNK)
        a = _dwconv_tile(scr_ref, a_ref[:, cols].astype(F32), ap_ref[:, cols].astype(F32),
                         an_ref[:, cols].astype(F32), cwa_ref[:, cols], cba_ref[:, cols], first, last)
        b = _dwconv_tile(scr_ref, b_ref[:, cols].astype(F32), bp_ref[:, cols].astype(F32),
                         bn_ref[:, cols].astype(F32), cwb_ref[:, cols], cbb_ref[:, cols], first, last)
        acc_ref[...] += _dot((_gelu_tanh(a) * b).astype(BF16), wd_ref[cols, :])
    y = acc_ref[...]
    o_ref[...] = _rms(y) * gf_ref[...] if final else y


def _ffn_tail(x, up, conv_w, conv_b, w_down, gf_row, geom, *, final):
    n, d = x.shape
    x_spec = pl.BlockSpec((TOKEN_TILE, d), lambda i: (i, 0))
    cwa, cwb = conv_w[:, :D_FF], conv_w[:, D_FF:]
    cba, cbb = conv_b[:D_FF].reshape(1, D_FF), conv_b[D_FF:].reshape(1, D_FF)
    return pl.pallas_call(
        functools.partial(_ffn_tail_body, geom=geom, final=final),
        out_shape=jax.ShapeDtypeStruct((n, d), F32), grid=(n // TOKEN_TILE,),
        in_specs=[x_spec] + _halo_specs(n, D_FF, 0) + _halo_specs(n, D_FF, 1)
        + [_resident(cwa.shape), _resident(cba.shape), _resident(cwb.shape), _resident(cbb.shape),
           _resident(w_down.shape), _resident((1, d))],
        out_specs=x_spec,
        scratch_shapes=[pltpu.VMEM((TOKEN_TILE, d), F32),
                        pltpu.VMEM((TOKEN_TILE + 2 * SUBLANES, FFN_COL_CHUNK), F32)],
        compiler_params=_cparams(("parallel",)), name="ffn_tail")(
            x, up, up, up, up, up, up, cwa, cba, cwb, cbb, w_down, gf_row)


def _pad_row(vec, offset):
    return jnp.zeros((1, LANES), F32).at[0, offset:offset + vec.shape[0]].set(vec.astype(F32))


def kernel(x_prompt, x_sample, norm_mix_g, w_in, na_rpb, ssm_conv_w, ssm_conv_b, ssm_dt_bias, ssm_a_log, ssm_d, ssm_norm_g, ml_igate_b, ml_fgate_b, ml_norm_g, sgu_norm_g, sgu_w, sgu_b, w_branch, w_gate, b_gate, w_out, norm_ffn_g, ffn_w_up, ffn_conv_w, ffn_conv_b, ffn_w_down, norm_final_g):
    bp, lp, d = x_prompt.shape
    bs, ls, _ = x_sample.shape
    assert d == D_MODEL and lp % TOKEN_TILE == 0 and ls % TOKEN_TILE == 0
    assert min(lp, ls) >= 2 * TOKEN_TILE
    geom = (bp * lp, lp, ls)
    depth = w_in.shape[0]
    x = jnp.concatenate([x_prompt.reshape(bp * lp, d), x_sample.reshape(bs * ls, d)], axis=0)

    o_ssm, o_ml, o_sgu = NA_COLS, NA_COLS + SSM_COLS, NA_COLS + SSM_COLS + ML_COLS
    bc_lo = o_ssm + 2 * BRANCH_WIDTH
    dt_lo = bc_lo + BC_W
    mlg_lo = o_ml + 4 * BRANCH_WIDTH
    for l in range(depth):
        wl = w_in[l]
        w_main = jnp.concatenate([wl[:, :bc_lo], wl[:, o_ml:mlg_lo], wl[:, o_sgu:], wl[:, bc_lo:dt_lo]],
                                 axis=1).astype(BF16)
        w_small = jnp.concatenate([wl[:, dt_lo:o_ml], wl[:, mlg_lo:o_sgu],
                                   jnp.zeros((d, LANES - 2 * SSM_HEADS - 4 * ML_HEADS), F32)],
                                  axis=1).astype(BF16)
        proj, small = _norm_matmul(x, norm_mix_g[l], w_main, w_small, name="in_proj")

        y_na = _neighbourhood_attention(proj, _na_bias_table(na_rpb[l]), geom)

        xc = _conv_silu(proj, PROJ_BLOCKS["ssm_x"], BRANCH_WIDTH, ssm_conv_w[l][:, :BRANCH_WIDTH],
                        ssm_conv_b[l][:BRANCH_WIDTH], geom, name="ssm_conv_x")
        bcc = _conv_silu(proj, BC_BLOCK, BC_W, ssm_conv_w[l][:, BRANCH_WIDTH:], ssm_conv_b[l][BRANCH_WIDTH:],
                         geom, name="ssm_conv_bc")
        dtb_row = _pad_row(ssm_dt_bias[l].reshape(-1), SMALL_DT)
        a_neg = -jnp.exp(ssm_a_log[l].astype(F32))
        y_f = _ssd(xc, bcc, small, dtb_row, _pad_row(a_neg[0], SMALL_DT), geom, reverse=False)
        d_row = jnp.repeat(ssm_d[l].astype(F32), SSM_HEAD_DIM).reshape(1, BRANCH_WIDTH)
        y_ssm = _ssd(xc, bcc, small, dtb_row, _pad_row(a_neg[1], SMALL_DT + SSM_HEADS), geom, reverse=True,
                     y_fwd=y_f, proj=proj, d_row=d_row, g_row=ssm_norm_g[l].reshape(1, BRANCH_WIDTH))

        gate_bias = _pad_row(jnp.concatenate([ml_igate_b[l].reshape(-1), ml_fgate_b[l].reshape(-1)]), SMALL_IG)
        h_f = _mlstm(proj, small, gate_bias, geom, reverse=False)
        y_ml = _mlstm(proj, small, gate_bias, geom, reverse=True, h_fwd=h_f,
                      g_row=ml_norm_g[l].reshape(1, BRANCH_WIDTH))

        sgu_b_cols = jnp.zeros((CHUNK, LANES), F32).at[:, :SGU_GROUPS].set(sgu_b[l].T.astype(F32))
        y_sgu = _sgu(proj, sgu_norm_g[l].reshape(1, BRANCH_WIDTH), sgu_w[l].astype(BF16), sgu_b_cols)

        x = _merge(x, norm_mix_g[l].reshape(1, d), w_gate[l].astype(BF16), b_gate[l].reshape(1, N_BRANCH * d),
                   (y_na, y_ssm, y_ml, y_sgu), w_branch[l].astype(BF16), w_out[l].astype(BF16))

        up = _norm_matmul(x, norm_ffn_g[l], ffn_w_up[l].astype(BF16), name="ffn_up")
        x = _ffn_tail(x, up, ffn_conv_w[l], ffn_conv_b[l], ffn_w_down[l].astype(BF16),
                      norm_final_g.reshape(1, d), geom, final=(l == depth - 1))

    return (x[:bp * lp].reshape(bp, lp, d), x[bp * lp:].reshape(bs, ls, d))
```

```python
import functools
import math

import jax
import jax.numpy as jnp
import numpy as np
from jax import lax
from jax.experimental import pallas as pl
from jax.experimental.pallas import tpu as pltpu

F32 = jnp.float32
BF16 = jnp.bfloat16

D_MODEL = 1024
GRID_W = 64
BRANCH_WIDTH = 512
N_BRANCH = 4
NA_HEADS = 8
NA_HEAD_DIM = BRANCH_WIDTH // NA_HEADS
NA_WR = 8
NA_WC = 16
SSM_HEADS = 8
SSM_HEAD_DIM = BRANCH_WIDTH // SSM_HEADS
SSM_GROUPS = 2
SSM_STATE = 64
SSM_CONV_W = 5
CHUNK = 128
XBC_W = BRANCH_WIDTH + 2 * SSM_GROUPS * SSM_STATE
ML_HEADS = 4
ML_HEAD_DIM = BRANCH_WIDTH // ML_HEADS
SGU_GROUPS = 4
SGU_GROUP_DIM = BRANCH_WIDTH // SGU_GROUPS
D_FF = 2816
FFN_CONV_W = 3
NORM_EPS = 1e-6
NA_COLS = 3 * BRANCH_WIDTH
SSM_COLS = BRANCH_WIDTH + XBC_W + 2 * SSM_HEADS
ML_COLS = 4 * BRANCH_WIDTH + 4 * ML_HEADS

LANES = 128
SUBLANES = 8
BF16_ROWS = 16
VMEM_LIMIT = 56 * 1024 * 1024

TOKEN_TILE = 512
N_STREAMS = 2
NA_BLOCK_ROWS = TOKEN_TILE // GRID_W
FFN_COL_CHUNK = 256
FFN_DOWN_GROUP = 512
FFN_ROW_BLOCK = 128
MASK_VALUE = -1e30

PROJ_BLOCKS = dict(na_q=0, na_k=1, na_v=2, ssm_z=3, ssm_x=4, ml_q=5, ml_k=6, ml_v=7, ml_o=8,
                   sgu_u=9, sgu_v=10)
PROJ_MAIN_W = 11 * BRANCH_WIDTH + 2 * SSM_GROUPS * SSM_STATE
BC_W = 2 * SSM_GROUPS * SSM_STATE
BC_BLOCK = (11 * BRANCH_WIDTH) // BC_W
SMALL_DT = 0
SMALL_IG = 2 * SSM_HEADS
SMALL_FG = SMALL_IG + 2 * ML_HEADS


def _cparams(sem):
    return pltpu.CompilerParams(dimension_semantics=sem, vmem_limit_bytes=VMEM_LIMIT)


def _resident(shape):
    nd = len(shape)
    return pl.BlockSpec(shape, lambda *_: (0,) * nd, pipeline_mode=pl.Buffered(1))


def _seg_pos(pos, geom):
    n_prompt, len_prompt, len_sample = geom
    in_prompt = pos < n_prompt
    local = jnp.where(in_prompt, pos % len_prompt, (pos - n_prompt) % len_sample)
    return local, jnp.where(in_prompt, len_prompt, len_sample)


def _rms(x):
    return x * lax.rsqrt(jnp.mean(x * x, axis=-1, keepdims=True) + NORM_EPS)


def _sigmoid(x):
    return 1.0 / (1.0 + jnp.exp(-x))


def _softplus(x):
    return jnp.maximum(x, 0.0) + jnp.log(1.0 + jnp.exp(-jnp.abs(x)))


def _gelu_tanh(x):
    return 0.5 * x * (1.0 + jnp.tanh(math.sqrt(2.0 / math.pi) * (x + 0.044715 * (x * x * x))))


def _dot(a, b):
    return jnp.dot(a, b, preferred_element_type=F32)


def _dot_nt(a, b):
    return lax.dot_general(a, b, (((1,), (1,)), ((), ())), preferred_element_type=F32)


def _cumsum_rows(tri, x):
    return jnp.dot(tri, x, preferred_element_type=F32, precision=lax.Precision.HIGHEST)


def _chunk_masks(reverse):
    t = lax.broadcasted_iota(jnp.int32, (CHUNK, CHUNK), 0)
    s = lax.broadcasted_iota(jnp.int32, (CHUNK, CHUNK), 1)
    mask = (s >= t) if reverse else (s <= t)
    return mask, jnp.where(mask, 1.0, 0.0).astype(F32)


def _norm_matmul_body(*refs, chunks, has_small):
    if has_small:
        x_ref, g_ref, w_ref, ws_ref, o_ref, os_ref = refs
    else:
        x_ref, g_ref, w_ref, o_ref = refs
    h = (_rms(x_ref[...]) * g_ref[...]).astype(BF16)
    for c0, cw in chunks:
        o_ref[:, c0:c0 + cw] = _dot(h, w_ref[:, c0:c0 + cw]).astype(o_ref.dtype)
    if has_small:
        os_ref[...] = _dot(h, ws_ref[...])


def _col_chunks(width, step=512):
    out, c = [], 0
    while c < width:
        out.append((c, min(step, width - c)))
        c += step
    return tuple(out)


def _norm_matmul(x, g, w, w_small=None, *, name):
    n, d = x.shape
    cols = w.shape[1]
    has_small = w_small is not None
    in_specs = [pl.BlockSpec((TOKEN_TILE, d), lambda i: (i, 0)), _resident((1, d)), _resident(w.shape)]
    out_shape = [jax.ShapeDtypeStruct((n, cols), BF16)]
    out_specs = [pl.BlockSpec((TOKEN_TILE, cols), lambda i: (i, 0))]
    args = [x, g.reshape(1, d), w]
    if has_small:
        in_specs.append(_resident(w_small.shape))
        out_shape.append(jax.ShapeDtypeStruct((n, w_small.shape[1]), F32))
        out_specs.append(pl.BlockSpec((TOKEN_TILE, w_small.shape[1]), lambda i: (i, 0)))
        args.append(w_small)
    res = pl.pallas_call(
        functools.partial(_norm_matmul_body, chunks=_col_chunks(cols), has_small=has_small),
        out_shape=out_shape, grid=(n // TOKEN_TILE,), in_specs=in_specs, out_specs=out_specs,
        compiler_params=_cparams(("parallel",)), name=name)(*args)
    return res if has_small else res[0]


def _na_bias_table(rpb):
    cols = np.arange(GRID_W)
    col_start = np.clip(cols - NA_WC // 2, 0, GRID_W - NA_WC)
    valid = (cols[None, :] >= col_start[:, None]) & (cols[None, :] < col_start[:, None] + NA_WC)
    dc = np.clip(cols[None, :] - cols[:, None] + (NA_WC - 1), 0, 2 * NA_WC - 2)
    dr = np.arange(NA_WR)[None, :] - np.arange(NA_WR)[:, None] + (NA_WR - 1)
    t = rpb.astype(F32)[:, dr][:, :, :, dc]
    t = jnp.where(jnp.asarray(valid)[None, None, None], t, MASK_VALUE)
    return t.transpose(1, 0, 3, 2, 4).reshape(NA_WR, NA_HEADS * GRID_W, NA_WR * GRID_W)


def _na_body(q_ref, kp_ref, kc_ref, kn_ref, vp_ref, vc_ref, vn_ref, bias_ref, o_ref, kbuf, vbuf, s_ref, p_ref,
             *, geom):
    local, seg = _seg_pos(pl.program_id(0) * TOKEN_TILE, geom)
    first = local == 0
    last = local + TOKEN_TILE == seg
    t = TOKEN_TILE
    kbuf[0:t, :] = kp_ref[...]
    kbuf[t:2 * t, :] = kc_ref[...]
    kbuf[2 * t:3 * t, :] = kn_ref[...]
    vbuf[0:t, :] = vp_ref[...]
    vbuf[t:2 * t, :] = vc_ref[...]
    vbuf[2 * t:3 * t, :] = vn_ref[...]
    lo = jnp.where(first, 0, -NA_WR)
    hi = jnp.where(last, 0, NA_WR)
    win = NA_WR * GRID_W
    low = lax.broadcasted_iota(jnp.int32, (GRID_W, LANES), 1) < NA_HEAD_DIM
    pair_rows = 2 * GRID_W

    def row_body(j, carry):
        start = NA_BLOCK_ROWS + jnp.clip(j - NA_WR // 2, lo, hi)
        off = NA_BLOCK_ROWS + j - start
        r0 = pl.multiple_of(start * GRID_W, GRID_W)
        q0 = pl.multiple_of(j * GRID_W, GRID_W)
        qrow = q_ref[pl.ds(q0, GRID_W), :].astype(F32) * NA_HEAD_DIM ** -0.5
        for hp in range(NA_HEADS // 2):
            cols = slice(hp * LANES, (hp + 1) * LANES)
            rows = slice(hp * pair_rows, (hp + 1) * pair_rows)
            qp = qrow[:, cols]
            q2 = jnp.concatenate([jnp.where(low, qp, 0.0), jnp.where(low, 0.0, qp)], axis=0).astype(BF16)
            s_ref[rows, :] = _dot_nt(q2, kbuf[pl.ds(r0, win), cols]) + bias_ref[off, rows, :]
        s = s_ref[...]
        p = jnp.exp(s - jnp.max(s, axis=-1, keepdims=True))
        inv = 1.0 / jnp.sum(p, axis=-1, keepdims=True)
        p_ref[...] = p.astype(BF16)
        outs = []
        for hp in range(NA_HEADS // 2):
            cols = slice(hp * LANES, (hp + 1) * LANES)
            rows = slice(hp * pair_rows, (hp + 1) * pair_rows)
            o2 = _dot(p_ref[rows, :], vbuf[pl.ds(r0, win), cols]) * inv[rows, :]
            outs.append(jnp.where(low, o2[:GRID_W, :], o2[GRID_W:, :]))
        o_ref[pl.ds(q0, GRID_W), :] = jnp.concatenate(outs, axis=1).astype(o_ref.dtype)
        return carry

    lax.fori_loop(0, NA_BLOCK_ROWS, row_body, 0)


def _neighbourhood_attention(proj, bias_table, geom):
    n = proj.shape[0]
    nb = n // TOKEN_TILE
    w = BRANCH_WIDTH

    def spec(col, shift):
        return pl.BlockSpec((TOKEN_TILE, w), lambda i: (jnp.clip(i + shift, 0, nb - 1), col))

    kq, kk, kv = PROJ_BLOCKS["na_q"], PROJ_BLOCKS["na_k"], PROJ_BLOCKS["na_v"]
    return pl.pallas_call(
        functools.partial(_na_body, geom=geom),
        out_shape=jax.ShapeDtypeStruct((n, w), BF16), grid=(nb,),
        in_specs=[spec(kq, 0), spec(kk, -1), spec(kk, 0), spec(kk, 1), spec(kv, -1), spec(kv, 0), spec(kv, 1),
                  _resident(bias_table.shape)],
        out_specs=pl.BlockSpec((TOKEN_TILE, w), lambda i: (i, 0)),
        scratch_shapes=[pltpu.VMEM((3 * TOKEN_TILE, w), BF16), pltpu.VMEM((3 * TOKEN_TILE, w), BF16),
                        pltpu.VMEM((NA_HEADS * GRID_W, NA_WR * GRID_W), F32),
                        pltpu.VMEM((NA_HEADS * GRID_W, NA_WR * GRID_W), BF16)],
        compiler_params=_cparams(("parallel",)), name="na_attention")(
            proj, proj, proj, proj, proj, proj, proj, bias_table)


def _dwconv_tile(scr_ref, cur, prev, nxt, w, bias, first, last):
    t = cur.shape[0]
    kw = w.shape[0]
    pad = kw // 2
    scr_ref[0:SUBLANES, :] = jnp.where(first, 0.0, prev[BF16_ROWS - SUBLANES:, :])
    scr_ref[SUBLANES:SUBLANES + t, :] = cur
    scr_ref[SUBLANES + t:2 * SUBLANES + t, :] = jnp.where(last, 0.0, nxt[:SUBLANES, :])
    acc = bias
    for k in range(kw):
        r = SUBLANES + k - pad
        acc = acc + w[k:k + 1, :] * scr_ref[r:r + t, :]
    return acc


def _halo_specs(n, width, col):
    per = TOKEN_TILE // BF16_ROWS
    nh = n // BF16_ROWS
    return [pl.BlockSpec((TOKEN_TILE, width), lambda i: (i, col)),
            pl.BlockSpec((BF16_ROWS, width), lambda i: (jnp.maximum(i * per - 1, 0), col)),
            pl.BlockSpec((BF16_ROWS, width), lambda i: (jnp.minimum((i + 1) * per, nh - 1), col))]


def _tile_ends(geom):
    local, seg = _seg_pos(pl.program_id(0) * TOKEN_TILE, geom)
    return local == 0, local + TOKEN_TILE == seg


def _conv_silu_body(c_ref, p_ref, n_ref, w_ref, b_ref, o_ref, scr_ref, *, geom):
    first, last = _tile_ends(geom)
    y = _dwconv_tile(scr_ref, c_ref[...].astype(F32), p_ref[...].astype(F32), n_ref[...].astype(F32),
                     w_ref[...], b_ref[...], first, last)
    o_ref[...] = (y * _sigmoid(y)).astype(o_ref.dtype)


def _conv_silu(proj, col, width, w, b, geom, *, name):
    n = proj.shape[0]
    return pl.pallas_call(
        functools.partial(_conv_silu_body, geom=geom),
        out_shape=jax.ShapeDtypeStruct((n, width), BF16), grid=(n // TOKEN_TILE,),
        in_specs=_halo_specs(n, width, col) + [_resident(w.shape), _resident((1, width))],
        out_specs=pl.BlockSpec((TOKEN_TILE, width), lambda i: (i, 0)),
        scratch_shapes=[pltpu.VMEM((TOKEN_TILE + 2 * SUBLANES, width), F32)],
        compiler_params=_cparams(("parallel",)), name=name)(proj, proj, proj, w, b.reshape(1, width))


def _ssd_body(*refs, geom, reverse, n_tiles, half_tokens):
    if reverse:
        x_ref, bc_ref, sm_ref, dtb_ref, a_ref, yf_ref, z_ref, d_ref, g_ref, o_ref, h_ref = refs
    else:
        x_ref, bc_ref, sm_ref, dtb_ref, a_ref, o_ref, h_ref = refs
    step = pl.program_id(0)
    tile = (n_tiles - 1 - step) if reverse else step
    mask, tri = _chunk_masks(reverse)
    dir_off = SMALL_DT + (SSM_HEADS if reverse else 0)
    hg = SSM_HEADS // SSM_GROUPS
    gw = hg * SSM_HEAD_DIM
    subs = TOKEN_TILE // CHUNK

    def prep(st, sub):
        local, seg = _seg_pos(st * half_tokens + tile * TOKEN_TILE + sub * CHUNK, geom)
        rows = pl.ds(pl.multiple_of(sub * CHUNK, CHUNK), CHUNK)
        bc = bc_ref[st, rows, :].astype(F32)
        dt = _softplus(sm_ref[st, rows, :] + dtb_ref[...])
        cs = _cumsum_rows(tri, dt * a_ref[...])
        return dict(st=st, rows=rows, start=(local + CHUNK == seg) if reverse else (local == 0),
                    bc=bc, bc_t=bc.T, dt=dt, dt_row=dt.T, cs=cs, cs_row=cs.T,
                    tot=cs[0:1, :] if reverse else cs[CHUNK - 1:CHUNK, :], ys=[])

    def group(c, g):
        st, rows, bc, cs, dt = c["st"], c["rows"], c["bc"], c["cs"], c["dt"]
        b_g = bc[:, g * SSM_STATE:(g + 1) * SSM_STATE].astype(BF16)
        c_lo = SSM_GROUPS * SSM_STATE + g * SSM_STATE
        c_g = bc[:, c_lo:c_lo + SSM_STATE].astype(BF16)
        b_gt = c["bc_t"][g * SSM_STATE:(g + 1) * SSM_STATE, :].astype(BF16)
        cb = _dot_nt(c_g, b_g)
        x = x_ref[st, rows, g * gw:(g + 1) * gw]
        xg = x.astype(F32)
        h_prev = jnp.where(c["start"], 0.0, h_ref[st, g])
        y_off = _dot(c_g, h_prev.astype(BF16))
        ys, xws, decs = [], [], []
        for j in range(hg):
            ch = dir_off + g * hg + j
            a_c = cs[:, ch:ch + 1]
            decay = jnp.exp(jnp.where(mask, a_c - c["cs_row"][ch:ch + 1, :], -jnp.inf))
            wts = (cb * decay * c["dt_row"][ch:ch + 1, :]).astype(BF16)
            hs = slice(j * SSM_HEAD_DIM, (j + 1) * SSM_HEAD_DIM)
            ys.append(_dot(wts, x[:, hs]) + y_off[:, hs] * jnp.exp(a_c))
            tot_h = c["tot"][:, ch:ch + 1]
            xws.append(xg[:, hs] * (jnp.exp(tot_h - a_c) * dt[:, ch:ch + 1]))
            decs.append(jnp.broadcast_to(jnp.exp(tot_h), (1, SSM_HEAD_DIM)))
        xw = jnp.concatenate(xws, axis=1).astype(BF16)
        h_ref[st, g] = h_prev * jnp.concatenate(decs, axis=1) + _dot(b_gt, xw)
        y = jnp.concatenate(ys, axis=1)
        if reverse:
            c["ys"].append(yf_ref[st, rows, g * gw:(g + 1) * gw] + y + xg * d_ref[:, g * gw:(g + 1) * gw])
        else:
            o_ref[st, rows, g * gw:(g + 1) * gw] = y

    def finish(c):
        st, rows = c["st"], c["rows"]
        y = jnp.concatenate(c["ys"], axis=1)
        z = z_ref[st, rows, :].astype(F32)
        o_ref[st, rows, :] = (_rms(y * (z * _sigmoid(z))) * g_ref[...]).astype(o_ref.dtype)

    def sub_body(it, carry):
        sub = (subs - 1 - it) if reverse else it
        for st in range(N_STREAMS):
            c = prep(st, sub)
            for g in range(SSM_GROUPS):
                group(c, g)
            if reverse:
                finish(c)
        return carry

    lax.fori_loop(0, subs, sub_body, 0)


def _streams(a):
    return a.reshape(N_STREAMS, a.shape[0] // N_STREAMS, a.shape[1])


def _stream_spec(nt, reverse):
    def tile_spec(width, col=0):
        if reverse:
            return pl.BlockSpec((N_STREAMS, TOKEN_TILE, width), lambda i: (0, nt - 1 - i, col))
        return pl.BlockSpec((N_STREAMS, TOKEN_TILE, width), lambda i: (0, i, col))
    return tile_spec


def _ssd(xc, bcc, small, dtb_row, a_row, geom, *, reverse, y_fwd=None, proj=None, d_row=None, g_row=None):
    half = xc.shape[1]
    nt = half // TOKEN_TILE
    w = BRANCH_WIDTH
    tile_spec = _stream_spec(nt, reverse)
    in_specs = [tile_spec(w), tile_spec(BC_W), tile_spec(LANES), _resident((1, LANES)), _resident((1, LANES))]
    args = [xc, bcc, small, dtb_row, a_row]
    if reverse:
        in_specs += [tile_spec(w), tile_spec(w, PROJ_BLOCKS["ssm_z"]), _resident((1, w)), _resident((1, w))]
        args += [y_fwd, proj, d_row, g_row]
    return pl.pallas_call(
        functools.partial(_ssd_body, geom=geom, reverse=reverse, n_tiles=nt, half_tokens=half),
        out_shape=jax.ShapeDtypeStruct((N_STREAMS, half, w), BF16 if reverse else F32), grid=(nt,),
        in_specs=in_specs, out_specs=tile_spec(w),
        scratch_shapes=[pltpu.VMEM((N_STREAMS, SSM_GROUPS, SSM_STATE, w // SSM_GROUPS), F32)],
        compiler_params=_cparams(("arbitrary",)), name="ssd_bwd" if reverse else "ssd_fwd")(*args)


def _mlstm_body(*refs, geom, reverse, n_tiles, half_tokens):
    if reverse:
        q_ref, k_ref, v_ref, sm_ref, gb_ref, hf_ref, og_ref, g_ref, o_ref, c_ref, n_ref, m_ref = refs
    else:
        q_ref, k_ref, v_ref, sm_ref, gb_ref, o_ref, c_ref, n_ref, m_ref = refs
    step = pl.program_id(0)
    tile = (n_tiles - 1 - step) if reverse else step
    mask, tri = _chunk_masks(reverse)
    ig_off = SMALL_IG + (ML_HEADS if reverse else 0)
    fg_off = SMALL_FG + (ML_HEADS if reverse else 0)
    subs = TOKEN_TILE // CHUNK
    k_scale = ML_HEAD_DIM ** -0.5

    def prep(st, sub):
        local, seg = _seg_pos(st * half_tokens + tile * TOKEN_TILE + sub * CHUNK, geom)
        rows = pl.ds(pl.multiple_of(sub * CHUNK, CHUNK), CHUNK)
        gates = sm_ref[st, rows, :] + gb_ref[...]
        fcs = _cumsum_rows(tri, -_softplus(-gates))
        return dict(st=st, rows=rows, start=(local + CHUNK == seg) if reverse else (local == 0),
                    gates=gates, fcs=fcs, fcs_row=fcs.T, gates_row=gates.T,
                    ftot=fcs[0:1, :] if reverse else fcs[CHUNK - 1:CHUNK, :])

    def head(c, h):
        st, rows, start = c["st"], c["rows"], c["start"]
        ci, cf = ig_off + h, fg_off + h
        hsl = slice(h * ML_HEAD_DIM, (h + 1) * ML_HEAD_DIM)
        fc = c["fcs"][:, cf:cf + 1]
        ig_c = c["gates"][:, ci:ci + 1]
        m_st = jnp.where(start, 0.0, m_ref[st, h:h + 1, 0:1])
        logd = jnp.where(mask, fc - c["fcs_row"][cf:cf + 1, :] + c["gates_row"][ci:ci + 1, :], -jnp.inf)
        m_inter = fc + m_st
        m_t = jnp.maximum(jnp.max(logd, axis=-1, keepdims=True), m_inter)
        qh = q_ref[st, rows, hsl]
        kh = k_ref[st, rows, hsl].astype(F32) * k_scale
        vh = v_ref[st, rows, hsl]
        s = _dot_nt(qh, kh.astype(BF16)) * jnp.exp(logd - m_t)
        sc = jnp.exp(m_inter - m_t)
        c_st = jnp.where(start, 0.0, c_ref[st, h])
        n_st = jnp.where(start, 0.0, n_ref[st, h:h + 1, :])
        num = _dot(s.astype(BF16), vh) + sc * _dot(qh, c_st.astype(BF16))
        den = (jnp.sum(s, axis=-1, keepdims=True)
               + sc * jnp.sum(qh.astype(F32) * n_st, axis=-1, keepdims=True))
        hh = num / jnp.maximum(jnp.abs(den), jnp.exp(-m_t))
        f_last = c["ftot"][:, cf:cf + 1]
        logw = f_last - fc + ig_c
        m_new = jnp.maximum(f_last + m_st, jnp.max(logw, axis=0, keepdims=True))
        kw = kh * jnp.exp(logw - m_new)
        dec = jnp.exp(f_last + m_st - m_new)
        c_ref[st, h] = dec * c_st + _dot(kw.T.astype(BF16), vh)
        n_ref[st, h:h + 1, :] = dec * n_st + jnp.sum(kw, axis=0, keepdims=True)
        m_ref[st, h:h + 1, :] = jnp.broadcast_to(m_new, (1, LANES))
        if reverse:
            hh = (_rms(hf_ref[st, rows, hsl] + hh) * g_ref[:, hsl]
                  * _sigmoid(og_ref[st, rows, hsl].astype(F32)))
        o_ref[st, rows, hsl] = hh.astype(o_ref.dtype)

    def sub_body(it, carry):
        sub = (subs - 1 - it) if reverse else it
        for st in range(N_STREAMS):
            c = prep(st, sub)
            for h in range(ML_HEADS):
                head(c, h)
        return carry

    lax.fori_loop(0, subs, sub_body, 0)


def _mlstm(proj, small, gate_bias_row, geom, *, reverse, h_fwd=None, g_row=None):
    half = proj.shape[1]
    nt = half // TOKEN_TILE
    w = BRANCH_WIDTH
    tile_spec = _stream_spec(nt, reverse)
    in_specs = [tile_spec(w, PROJ_BLOCKS["ml_q"]), tile_spec(w, PROJ_BLOCKS["ml_k"]),
                tile_spec(w, PROJ_BLOCKS["ml_v"]), tile_spec(LANES), _resident((1, LANES))]
    args = [proj, proj, proj, small, gate_bias_row]
    if reverse:
        in_specs += [tile_spec(w), tile_spec(w, PROJ_BLOCKS["ml_o"]), _resident((1, w))]
        args += [h_fwd, proj, g_row]
    return pl.pallas_call(
        functools.partial(_mlstm_body, geom=geom, reverse=reverse, n_tiles=nt, half_tokens=half),
        out_shape=jax.ShapeDtypeStruct((N_STREAMS, half, w), BF16 if reverse else F32), grid=(nt,),
        in_specs=in_specs, out_specs=tile_spec(w),
        scratch_shapes=[pltpu.VMEM((N_STREAMS, ML_HEADS, ML_HEAD_DIM, ML_HEAD_DIM), F32),
                        pltpu.VMEM((N_STREAMS, SUBLANES, ML_HEAD_DIM), F32),
                        pltpu.VMEM((N_STREAMS, SUBLANES, LANES), F32)],
        compiler_params=_cparams(("arbitrary",)), name="mlstm_bwd" if reverse else "mlstm_fwd")(*args)


def _sgu_body(u_ref, v_ref, g_ref, w_ref, b_ref, o_ref):
    u = _gelu_tanh(u_ref[...].astype(F32))
    vg = (_rms(_gelu_tanh(v_ref[...].astype(F32))) * g_ref[...]).astype(BF16)
    for c in range(TOKEN_TILE // CHUNK):
        rows = slice(c * CHUNK, (c + 1) * CHUNK)
        outs = []
        for g in range(SGU_GROUPS):
            cols = slice(g * SGU_GROUP_DIM, (g + 1) * SGU_GROUP_DIM)
            outs.append(_dot(w_ref[g], vg[rows, cols]) + b_ref[:, g:g + 1])
        o_ref[rows, :] = (u[rows, :] * jnp.concatenate(outs, axis=1)).astype(o_ref.dtype)


def _sgu(proj, g_row, w, b_cols):
    n = proj.shape[0]
    wd = BRANCH_WIDTH
    return pl.pallas_call(
        _sgu_body, out_shape=jax.ShapeDtypeStruct((n, wd), BF16), grid=(n // TOKEN_TILE,),
        in_specs=[pl.BlockSpec((TOKEN_TILE, wd), lambda i: (i, PROJ_BLOCKS["sgu_u"])),
                  pl.BlockSpec((TOKEN_TILE, wd), lambda i: (i, PROJ_BLOCKS["sgu_v"])),
                  _resident((1, wd)), _resident(w.shape), _resident(b_cols.shape)],
        out_specs=pl.BlockSpec((TOKEN_TILE, wd), lambda i: (i, 0)),
        compiler_params=_cparams(("parallel",)), name="sgu")(proj, proj, g_row, w, b_cols)


def _merge_body(x_ref, gm_ref, wg_ref, bg_ref, y0_ref, y1_ref, y2_ref, y3_ref, wb_ref, wo_ref, o_ref):
    x = x_ref[...]
    h = (_rms(x) * gm_ref[...]).astype(BF16)
    merged = None
    for i, y_ref in enumerate((y0_ref, y1_ref, y2_ref, y3_ref)):
        cols = slice(i * D_MODEL, (i + 1) * D_MODEL)
        term = _sigmoid(_dot(h, wg_ref[:, cols]) + bg_ref[:, cols]) * _dot(y_ref[...], wb_ref[i])
        merged = term if merged is None else merged + term
    o_ref[...] = x + _dot(merged.astype(BF16), wo_ref[...])


def _merge(x, gm_row, wg, bg_row, branches, wb, wo):
    n, d = x.shape
    br_spec = pl.BlockSpec((TOKEN_TILE, BRANCH_WIDTH), lambda i: (i, 0))
    x_spec = pl.BlockSpec((TOKEN_TILE, d), lambda i: (i, 0))
    return pl.pallas_call(
        _merge_body, out_shape=jax.ShapeDtypeStruct((n, d), F32), grid=(n // TOKEN_TILE,),
        in_specs=[x_spec, _resident((1, d)), _resident(wg.shape), _resident(bg_row.shape),
                  br_spec, br_spec, br_spec, br_spec, _resident(wb.shape), _resident(wo.shape)],
        out_specs=x_spec, compiler_params=_cparams(("parallel",)), name="merge")(
            x, gm_row, wg, bg_row, *branches, wb, wo)


def _shift_matrix():
    rb = FFN_ROW_BLOCK
    s = np.zeros((2 * rb, rb + 2 * BF16_ROWS), np.float32)
    i = np.arange(rb)
    s[i, i + BF16_ROWS - 1] = 1.0
    s[rb + i, i + BF16_ROWS + 1] = 1.0
    return jnp.asarray(s, BF16)


def _ffn_tail_body(x_ref, a_ref, ap_ref, an_ref, b_ref, bp_ref, bn_ref, sh_ref, cwa_ref, cba_ref, cwb_ref,
                   cbb_ref, wd_ref, gf_ref, o_ref, act_ref, *, geom, final):
    first, last = _tile_ends(geom)
    shift = sh_ref[...]
    rb = FFN_ROW_BLOCK
    nblk = TOKEN_TILE // rb

    def conv3(c_ref, p_ref, n_ref, w_ref, bias_ref, cols, blk):
        r0 = blk * rb
        if blk == 0:
            before = p_ref[:, cols]
            before = jnp.where(first, jnp.zeros_like(before), before)
        else:
            before = c_ref[r0 - BF16_ROWS:r0, cols]
        if blk == nblk - 1:
            after = n_ref[:, cols]
            after = jnp.where(last, jnp.zeros_like(after), after)
        else:
            after = c_ref[r0 + rb:r0 + rb + BF16_ROWS, cols]
        mid = c_ref[r0:r0 + rb, cols]
        nb = _dot(shift, jnp.concatenate([before, mid, after], axis=0))
        return (bias_ref[:, cols] + w_ref[0:1, cols] * nb[:rb, :] + w_ref[1:2, cols] * mid.astype(F32)
                + w_ref[2:3, cols] * nb[rb:, :])

    o_ref[...] = x_ref[...]
    for g0 in range(0, D_FF, FFN_DOWN_GROUP):
        g1 = min(g0 + FFN_DOWN_GROUP, D_FF)
        for c0 in range(g0, g1, FFN_COL_CHUNK):
            cols = slice(c0, c0 + FFN_COL_CHUNK)
            for blk in range(nblk):
                a = conv3(a_ref, ap_ref, an_ref, cwa_ref, cba_ref, cols, blk)
                b = conv3(b_ref, bp_ref, bn_ref, cwb_ref, cbb_ref, cols, blk)
                act_ref[blk * rb:(blk + 1) * rb, cols] = (_gelu_tanh(a) * b).astype(BF16)
        o_ref[...] += _dot(act_ref[:, g0:g1], wd_ref[g0:g1, :])
    if final:
        o_ref[...] = _rms(o_ref[...]) * gf_ref[...]


def _ffn_tail(x, up, conv_w, conv_b, w_down, gf_row, geom, *, final):
    assert conv_w.shape[0] == FFN_CONV_W == 3
    n, d = x.shape
    x_spec = pl.BlockSpec((TOKEN_TILE, d), lambda i: (i, 0))
    cwa, cwb = conv_w[:, :D_FF], conv_w[:, D_FF:]
    cba, cbb = conv_b[:D_FF].reshape(1, D_FF), conv_b[D_FF:].reshape(1, D_FF)
    shift = _shift_matrix()
    return pl.pallas_call(
        functools.partial(_ffn_tail_body, geom=geom, final=final),
        out_shape=jax.ShapeDtypeStruct((n, d), F32), grid=(n // TOKEN_TILE,),
        in_specs=[x_spec] + _halo_specs(n, D_FF, 0) + _halo_specs(n, D_FF, 1)
        + [_resident(shift.shape), _resident(cwa.shape), _resident(cba.shape), _resident(cwb.shape),
           _resident(cbb.shape), _resident(w_down.shape), _resident((1, d))],
        out_specs=x_spec,
        scratch_shapes=[pltpu.VMEM((TOKEN_TILE, D_FF), BF16)],
        compiler_params=_cparams(("parallel",)), name="ffn_tail")(
            x, up, up, up, up, up, up, shift, cwa, cba, cwb, cbb, w_down, gf_row)


def _pad_row(vec, offset):
    return jnp.zeros((1, LANES), F32).at[0, offset:offset + vec.shape[0]].set(vec.astype(F32))


def kernel(x_prompt, x_sample, norm_mix_g, w_in, na_rpb, ssm_conv_w, ssm_conv_b, ssm_dt_bias, ssm_a_log, ssm_d, ssm_norm_g, ml_igate_b, ml_fgate_b, ml_norm_g, sgu_norm_g, sgu_w, sgu_b, w_branch, w_gate, b_gate, w_out, norm_ffn_g, ffn_w_up, ffn_conv_w, ffn_conv_b, ffn_w_down, norm_final_g):
    bp, lp, d = x_prompt.shape
    bs, ls, _ = x_sample.shape
    assert d == D_MODEL and lp % TOKEN_TILE == 0 and ls % TOKEN_TILE == 0
    assert min(lp, ls) >= 2 * TOKEN_TILE
    geom = (bp * lp, lp, ls)
    n = bp * lp + bs * ls
    for st in range(1, N_STREAMS):
        split = st * n // N_STREAMS
        assert n % N_STREAMS == 0 and (split % lp == 0 if split <= bp * lp else (split - bp * lp) % ls == 0)
    depth = w_in.shape[0]
    x = jnp.concatenate([x_prompt.reshape(bp * lp, d), x_sample.reshape(bs * ls, d)], axis=0)

    o_ssm, o_ml, o_sgu = NA_COLS, NA_COLS + SSM_COLS, NA_COLS + SSM_COLS + ML_COLS
    bc_lo = o_ssm + 2 * BRANCH_WIDTH
    dt_lo = bc_lo + BC_W
    mlg_lo = o_ml + 4 * BRANCH_WIDTH
    for l in range(depth):
        wl = w_in[l]
        w_main = jnp.concatenate([wl[:, :bc_lo], wl[:, o_ml:mlg_lo], wl[:, o_sgu:], wl[:, bc_lo:dt_lo]],
                                 axis=1).astype(BF16)
        w_small = jnp.concatenate([wl[:, dt_lo:o_ml], wl[:, mlg_lo:o_sgu],
                                   jnp.zeros((d, LANES - 2 * SSM_HEADS - 4 * ML_HEADS), F32)],
                                  axis=1).astype(BF16)
        proj, small = _norm_matmul(x, norm_mix_g[l], w_main, w_small, name="in_proj")

        y_na = _neighbourhood_attention(proj, _na_bias_table(na_rpb[l]), geom)

        xc = _conv_silu(proj, PROJ_BLOCKS["ssm_x"], BRANCH_WIDTH, ssm_conv_w[l][:, :BRANCH_WIDTH],
                        ssm_conv_b[l][:BRANCH_WIDTH], geom, name="ssm_conv_x")
        bcc = _conv_silu(proj, BC_BLOCK, BC_W, ssm_conv_w[l][:, BRANCH_WIDTH:], ssm_conv_b[l][BRANCH_WIDTH:],
                         geom, name="ssm_conv_bc")
        dtb_row = _pad_row(ssm_dt_bias[l].reshape(-1), SMALL_DT)
        a_neg = -jnp.exp(ssm_a_log[l].astype(F32))
        proj_s, small_s, xc_s, bcc_s = _streams(proj), _streams(small), _streams(xc), _streams(bcc)
        y_f = _ssd(xc_s, bcc_s, small_s, dtb_row, _pad_row(a_neg[0], SMALL_DT), geom, reverse=False)
        d_row = jnp.repeat(ssm_d[l].astype(F32), SSM_HEAD_DIM).reshape(1, BRANCH_WIDTH)
        y_ssm = _ssd(xc_s, bcc_s, small_s, dtb_row, _pad_row(a_neg[1], SMALL_DT + SSM_HEADS), geom, reverse=True,
                     y_fwd=y_f, proj=proj_s, d_row=d_row, g_row=ssm_norm_g[l].reshape(1, BRANCH_WIDTH))
        y_ssm = y_ssm.reshape(n, BRANCH_WIDTH)

        gate_bias = _pad_row(jnp.concatenate([ml_igate_b[l].reshape(-1), ml_fgate_b[l].reshape(-1)]), SMALL_IG)
        h_f = _mlstm(proj_s, small_s, gate_bias, geom, reverse=False)
        y_ml = _mlstm(proj_s, small_s, gate_bias, geom, reverse=True, h_fwd=h_f,
                      g_row=ml_norm_g[l].reshape(1, BRANCH_WIDTH)).reshape(n, BRANCH_WIDTH)

        sgu_b_cols = jnp.zeros((CHUNK, LANES), F32).at[:, :SGU_GROUPS].set(sgu_b[l].T.astype(F32))
        y_sgu = _sgu(proj, sgu_norm_g[l].reshape(1, BRANCH_WIDTH), sgu_w[l].astype(BF16), sgu_b_cols)

        x = _merge(x, norm_mix_g[l].reshape(1, d), w_gate[l].astype(BF16), b_gate[l].reshape(1, N_BRANCH * d),
                   (y_na, y_ssm, y_ml, y_sgu), w_branch[l].astype(BF16), w_out[l].astype(BF16))

        up = _norm_matmul(x, norm_ffn_g[l], ffn_w_up[l].astype(BF16), name="ffn_up")
        x = _ffn_tail(x, up, ffn_conv_w[l], ffn_conv_b[l], ffn_w_down[l].astype(BF16),
                      norm_final_g.reshape(1, d), geom, final=(l == depth - 1))

    return (x[:bp * lp].reshape(bp, lp, d), x[bp * lp:].reshape(bs, ls, d))
```

```python
import functools
import math

import jax
import jax.numpy as jnp
import numpy as np
from jax import lax
from jax.experimental import pallas as pl
from jax.experimental.pallas import tpu as pltpu

F32 = jnp.float32
BF16 = jnp.bfloat16

D_MODEL = 1024
GRID_W = 64
BRANCH_WIDTH = 512
N_BRANCH = 4
NA_HEADS = 8
NA_HEAD_DIM = BRANCH_WIDTH // NA_HEADS
NA_WR = 8
NA_WC = 16
SSM_HEADS = 8
SSM_HEAD_DIM = BRANCH_WIDTH // SSM_HEADS
SSM_GROUPS = 2
SSM_STATE = 64
SSM_CONV_W = 5
CHUNK = 128
XBC_W = BRANCH_WIDTH + 2 * SSM_GROUPS * SSM_STATE
ML_HEADS = 4
ML_HEAD_DIM = BRANCH_WIDTH // ML_HEADS
SGU_GROUPS = 4
SGU_GROUP_DIM = BRANCH_WIDTH // SGU_GROUPS
D_FF = 2816
FFN_CONV_W = 3
NORM_EPS = 1e-6
NA_COLS = 3 * BRANCH_WIDTH
SSM_COLS = BRANCH_WIDTH + XBC_W + 2 * SSM_HEADS
ML_COLS = 4 * BRANCH_WIDTH + 4 * ML_HEADS

LANES = 128
SUBLANES = 8
BF16_ROWS = 16
VMEM_LIMIT = 56 * 1024 * 1024

TOKEN_TILE = 512
N_STREAMS = 2
NA_BLOCK_ROWS = TOKEN_TILE // GRID_W
FFN_COL_CHUNK = 256
FFN_DOWN_GROUP = 512
MASK_VALUE = -1e30

PROJ_BLOCKS = dict(na_q=0, na_k=1, na_v=2, ssm_z=3, ssm_x=4, ml_q=5, ml_k=6, ml_v=7, ml_o=8,
                   sgu_u=9, sgu_v=10)
PROJ_MAIN_W = 11 * BRANCH_WIDTH + 2 * SSM_GROUPS * SSM_STATE
BC_W = 2 * SSM_GROUPS * SSM_STATE
BC_BLOCK = (11 * BRANCH_WIDTH) // BC_W
SMALL_DT = 0
SMALL_IG = 2 * SSM_HEADS
SMALL_FG = SMALL_IG + 2 * ML_HEADS


def _cparams(sem):
    return pltpu.CompilerParams(dimension_semantics=sem, vmem_limit_bytes=VMEM_LIMIT)


def _resident(shape):
    nd = len(shape)
    return pl.BlockSpec(shape, lambda *_: (0,) * nd, pipeline_mode=pl.Buffered(1))


def _seg_pos(pos, geom):
    n_prompt, len_prompt, len_sample = geom
    in_prompt = pos < n_prompt
    local = jnp.where(in_prompt, pos % len_prompt, (pos - n_prompt) % len_sample)
    return local, jnp.where(in_prompt, len_prompt, len_sample)


def _rms(x):
    return x * lax.rsqrt(jnp.mean(x * x, axis=-1, keepdims=True) + NORM_EPS)


def _sigmoid(x):
    return 1.0 / (1.0 + jnp.exp(-x))


def _softplus(x):
    return jnp.maximum(x, 0.0) + jnp.log(1.0 + jnp.exp(-jnp.abs(x)))


def _gelu_tanh(x):
    return 0.5 * x * (1.0 + jnp.tanh(math.sqrt(2.0 / math.pi) * (x + 0.044715 * (x * x * x))))


def _dot(a, b):
    return jnp.dot(a, b, preferred_element_type=F32)


def _dot_nt(a, b):
    return lax.dot_general(a, b, (((1,), (1,)), ((), ())), preferred_element_type=F32)


def _cumsum_rows(tri, x):
    return jnp.dot(tri, x, preferred_element_type=F32, precision=lax.Precision.HIGHEST)


def _chunk_masks(reverse):
    t = lax.broadcasted_iota(jnp.int32, (CHUNK, CHUNK), 0)
    s = lax.broadcasted_iota(jnp.int32, (CHUNK, CHUNK), 1)
    mask = (s >= t) if reverse else (s <= t)
    return mask, jnp.where(mask, 1.0, 0.0).astype(F32)


def _norm_matmul_body(*refs, chunks, has_small):
    if has_small:
        x_ref, g_ref, w_ref, ws_ref, o_ref, os_ref = refs
    else:
        x_ref, g_ref, w_ref, o_ref = refs
    h = (_rms(x_ref[...]) * g_ref[...]).astype(BF16)
    for c0, cw in chunks:
        o_ref[:, c0:c0 + cw] = _dot(h, w_ref[:, c0:c0 + cw]).astype(o_ref.dtype)
    if has_small:
        os_ref[...] = _dot(h, ws_ref[...])


def _col_chunks(width, step=512):
    out, c = [], 0
    while c < width:
        out.append((c, min(step, width - c)))
        c += step
    return tuple(out)


def _norm_matmul(x, g, w, w_small=None, *, name):
    n, d = x.shape
    cols = w.shape[1]
    has_small = w_small is not None
    in_specs = [pl.BlockSpec((TOKEN_TILE, d), lambda i: (i, 0)), _resident((1, d)), _resident(w.shape)]
    out_shape = [jax.ShapeDtypeStruct((n, cols), BF16)]
    out_specs = [pl.BlockSpec((TOKEN_TILE, cols), lambda i: (i, 0))]
    args = [x, g.reshape(1, d), w]
    if has_small:
        in_specs.append(_resident(w_small.shape))
        out_shape.append(jax.ShapeDtypeStruct((n, w_small.shape[1]), F32))
        out_specs.append(pl.BlockSpec((TOKEN_TILE, w_small.shape[1]), lambda i: (i, 0)))
        args.append(w_small)
    res = pl.pallas_call(
        functools.partial(_norm_matmul_body, chunks=_col_chunks(cols), has_small=has_small),
        out_shape=out_shape, grid=(n // TOKEN_TILE,), in_specs=in_specs, out_specs=out_specs,
        compiler_params=_cparams(("parallel",)), name=name)(*args)
    return res if has_small else res[0]


def _na_bias_table(rpb):
    cols = np.arange(GRID_W)
    col_start = np.clip(cols - NA_WC // 2, 0, GRID_W - NA_WC)
    valid = (cols[None, :] >= col_start[:, None]) & (cols[None, :] < col_start[:, None] + NA_WC)
    dc = np.clip(cols[None, :] - cols[:, None] + (NA_WC - 1), 0, 2 * NA_WC - 2)
    dr = np.arange(NA_WR)[None, :] - np.arange(NA_WR)[:, None] + (NA_WR - 1)
    t = rpb.astype(F32)[:, dr][:, :, :, dc]
    t = jnp.where(jnp.asarray(valid)[None, None, None], t, MASK_VALUE)
    return t.transpose(1, 0, 3, 2, 4).reshape(NA_WR, NA_HEADS * GRID_W, NA_WR * GRID_W)


def _na_body(q_ref, kp_ref, kc_ref, kn_ref, vp_ref, vc_ref, vn_ref, bias_ref, o_ref, kbuf, vbuf, s_ref, p_ref,
             *, geom):
    local, seg = _seg_pos(pl.program_id(0) * TOKEN_TILE, geom)
    first = local == 0
    last = local + TOKEN_TILE == seg
    t = TOKEN_TILE
    kbuf[0:t, :] = kp_ref[...]
    kbuf[t:2 * t, :] = kc_ref[...]
    kbuf[2 * t:3 * t, :] = kn_ref[...]
    vbuf[0:t, :] = vp_ref[...]
    vbuf[t:2 * t, :] = vc_ref[...]
    vbuf[2 * t:3 * t, :] = vn_ref[...]
    lo = jnp.where(first, 0, -NA_WR)
    hi = jnp.where(last, 0, NA_WR)
    win = NA_WR * GRID_W
    low = lax.broadcasted_iota(jnp.int32, (GRID_W, LANES), 1) < NA_HEAD_DIM
    pair_rows = 2 * GRID_W

    def row_body(j, carry):
        start = NA_BLOCK_ROWS + jnp.clip(j - NA_WR // 2, lo, hi)
        off = NA_BLOCK_ROWS + j - start
        r0 = pl.multiple_of(start * GRID_W, GRID_W)
        q0 = pl.multiple_of(j * GRID_W, GRID_W)
        qrow = q_ref[pl.ds(q0, GRID_W), :].astype(F32) * NA_HEAD_DIM ** -0.5
        for hp in range(NA_HEADS // 2):
            cols = slice(hp * LANES, (hp + 1) * LANES)
            rows = slice(hp * pair_rows, (hp + 1) * pair_rows)
            qp = qrow[:, cols]
            q2 = jnp.concatenate([jnp.where(low, qp, 0.0), jnp.where(low, 0.0, qp)], axis=0).astype(BF16)
            s_ref[rows, :] = _dot_nt(q2, kbuf[pl.ds(r0, win), cols]) + bias_ref[off, rows, :]
        s = s_ref[...]
        p = jnp.exp(s - jnp.max(s, axis=-1, keepdims=True))
        inv = 1.0 / jnp.sum(p, axis=-1, keepdims=True)
        p_ref[...] = p.astype(BF16)
        outs = []
        for hp in range(NA_HEADS // 2):
            cols = slice(hp * LANES, (hp + 1) * LANES)
            rows = slice(hp * pair_rows, (hp + 1) * pair_rows)
            o2 = _dot(p_ref[rows, :], vbuf[pl.ds(r0, win), cols]) * inv[rows, :]
            outs.append(jnp.where(low, o2[:GRID_W, :], o2[GRID_W:, :]))
        o_ref[pl.ds(q0, GRID_W), :] = jnp.concatenate(outs, axis=1).astype(o_ref.dtype)
        return carry

    lax.fori_loop(0, NA_BLOCK_ROWS, row_body, 0)


def _neighbourhood_attention(proj, bias_table, geom):
    n = proj.shape[0]
    nb = n // TOKEN_TILE
    w = BRANCH_WIDTH

    def spec(col, shift):
        return pl.BlockSpec((TOKEN_TILE, w), lambda i: (jnp.clip(i + shift, 0, nb - 1), col))

    kq, kk, kv = PROJ_BLOCKS["na_q"], PROJ_BLOCKS["na_k"], PROJ_BLOCKS["na_v"]
    return pl.pallas_call(
        functools.partial(_na_body, geom=geom),
        out_shape=jax.ShapeDtypeStruct((n, w), BF16), grid=(nb,),
        in_specs=[spec(kq, 0), spec(kk, -1), spec(kk, 0), spec(kk, 1), spec(kv, -1), spec(kv, 0), spec(kv, 1),
                  _resident(bias_table.shape)],
        out_specs=pl.BlockSpec((TOKEN_TILE, w), lambda i: (i, 0)),
        scratch_shapes=[pltpu.VMEM((3 * TOKEN_TILE, w), BF16), pltpu.VMEM((3 * TOKEN_TILE, w), BF16),
                        pltpu.VMEM((NA_HEADS * GRID_W, NA_WR * GRID_W), F32),
                        pltpu.VMEM((NA_HEADS * GRID_W, NA_WR * GRID_W), BF16)],
        compiler_params=_cparams(("parallel",)), name="na_attention")(
            proj, proj, proj, proj, proj, proj, proj, bias_table)


def _dwconv_tile(scr_ref, cur, prev, nxt, w, bias, first, last):
    t = cur.shape[0]
    kw = w.shape[0]
    pad = kw // 2
    scr_ref[0:SUBLANES, :] = jnp.where(first, 0.0, prev[BF16_ROWS - SUBLANES:, :])
    scr_ref[SUBLANES:SUBLANES + t, :] = cur
    scr_ref[SUBLANES + t:2 * SUBLANES + t, :] = jnp.where(last, 0.0, nxt[:SUBLANES, :])
    acc = bias
    for k in range(kw):
        r = SUBLANES + k - pad
        acc = acc + w[k:k + 1, :] * scr_ref[r:r + t, :]
    return acc


def _halo_specs(n, width, col):
    per = TOKEN_TILE // BF16_ROWS
    nh = n // BF16_ROWS
    return [pl.BlockSpec((TOKEN_TILE, width), lambda i: (i, col)),
            pl.BlockSpec((BF16_ROWS, width), lambda i: (jnp.maximum(i * per - 1, 0), col)),
            pl.BlockSpec((BF16_ROWS, width), lambda i: (jnp.minimum((i + 1) * per, nh - 1), col))]


def _tile_ends(geom):
    local, seg = _seg_pos(pl.program_id(0) * TOKEN_TILE, geom)
    return local == 0, local + TOKEN_TILE == seg


def _conv_silu_body(c_ref, p_ref, n_ref, w_ref, b_ref, o_ref, scr_ref, *, geom):
    first, last = _tile_ends(geom)
    y = _dwconv_tile(scr_ref, c_ref[...].astype(F32), p_ref[...].astype(F32), n_ref[...].astype(F32),
                     w_ref[...], b_ref[...], first, last)
    o_ref[...] = (y * _sigmoid(y)).astype(o_ref.dtype)


def _conv_silu(proj, col, width, w, b, geom, *, name):
    n = proj.shape[0]
    return pl.pallas_call(
        functools.partial(_conv_silu_body, geom=geom),
        out_shape=jax.ShapeDtypeStruct((n, width), BF16), grid=(n // TOKEN_TILE,),
        in_specs=_halo_specs(n, width, col) + [_resident(w.shape), _resident((1, width))],
        out_specs=pl.BlockSpec((TOKEN_TILE, width), lambda i: (i, 0)),
        scratch_shapes=[pltpu.VMEM((TOKEN_TILE + 2 * SUBLANES, width), F32)],
        compiler_params=_cparams(("parallel",)), name=name)(proj, proj, proj, w, b.reshape(1, width))


def _ssd_body(*refs, geom, reverse, n_tiles, half_tokens):
    if reverse:
        x_ref, bc_ref, sm_ref, dtb_ref, a_ref, yf_ref, z_ref, d_ref, g_ref, o_ref, h_ref = refs
    else:
        x_ref, bc_ref, sm_ref, dtb_ref, a_ref, o_ref, h_ref = refs
    step = pl.program_id(0)
    tile = (n_tiles - 1 - step) if reverse else step
    mask, tri = _chunk_masks(reverse)
    dir_off = SMALL_DT + (SSM_HEADS if reverse else 0)
    hg = SSM_HEADS // SSM_GROUPS
    gw = hg * SSM_HEAD_DIM
    subs = TOKEN_TILE // CHUNK

    def prep(st, sub):
        local, seg = _seg_pos(st * half_tokens + tile * TOKEN_TILE + sub * CHUNK, geom)
        rows = pl.ds(pl.multiple_of(sub * CHUNK, CHUNK), CHUNK)
        bc = bc_ref[st, rows, :].astype(F32)
        dt = _softplus(sm_ref[st, rows, :] + dtb_ref[...])
        cs = _cumsum_rows(tri, dt * a_ref[...])
        return dict(st=st, rows=rows, start=(local + CHUNK == seg) if reverse else (local == 0),
                    bc=bc, bc_t=bc.T, dt=dt, dt_row=dt.T, cs=cs, cs_row=cs.T,
                    tot=cs[0:1, :] if reverse else cs[CHUNK - 1:CHUNK, :], ys=[])

    def group(c, g):
        st, rows, bc, cs, dt = c["st"], c["rows"], c["bc"], c["cs"], c["dt"]
        b_g = bc[:, g * SSM_STATE:(g + 1) * SSM_STATE].astype(BF16)
        c_lo = SSM_GROUPS * SSM_STATE + g * SSM_STATE
        c_g = bc[:, c_lo:c_lo + SSM_STATE].astype(BF16)
        b_gt = c["bc_t"][g * SSM_STATE:(g + 1) * SSM_STATE, :]
        cb = _dot_nt(c_g, b_g)
        x = x_ref[st, rows, g * gw:(g + 1) * gw]
        h_prev = jnp.where(c["start"], 0.0, h_ref[st, g])
        y_off = _dot(c_g, h_prev.astype(BF16))
        low = lax.broadcasted_iota(jnp.int32, (CHUNK, LANES), 1) < SSM_HEAD_DIM
        head_of_lane = lax.broadcasted_iota(jnp.int32, (1, gw), 1) // SSM_HEAD_DIM
        y_pairs, lhs, dec = [], [], jnp.zeros((1, gw), F32)
        for p in range(hg // 2):
            ps = slice(p * LANES, (p + 1) * LANES)
            yd, ea = [], []
            for j in (2 * p, 2 * p + 1):
                ch = dir_off + g * hg + j
                cs_r = c["cs_row"][ch:ch + 1, :]
                dt_r = c["dt_row"][ch:ch + 1, :]
                a_rep = jnp.broadcast_to(cs[:, ch:ch + 1], (CHUNK, CHUNK))
                decay = jnp.exp(jnp.where(mask, a_rep - cs_r, -jnp.inf))
                yd.append(_dot((cb * decay * dt_r).astype(BF16), x[:, ps]))
                ea.append(jnp.exp(a_rep))
                tot_h = jnp.broadcast_to(c["tot"][:, ch:ch + 1], (1, CHUNK))
                lhs.append(b_gt * (jnp.exp(tot_h - cs_r) * dt_r))
                dec = jnp.where(head_of_lane == j, jnp.exp(c["tot"][:, ch:ch + 1]), dec)
            y_pairs.append(jnp.where(low, yd[0], yd[1]) + y_off[:, ps] * jnp.where(low, ea[0], ea[1]))
        upd = _dot(jnp.concatenate(lhs, axis=0).astype(BF16), x)
        new = jnp.zeros((SSM_STATE, gw), F32)
        for j in range(hg):
            new = jnp.where(head_of_lane == j, upd[j * SSM_STATE:(j + 1) * SSM_STATE, :], new)
        h_ref[st, g] = h_prev * dec + new
        y = jnp.concatenate(y_pairs, axis=1)
        if reverse:
            c["ys"].append(yf_ref[st, rows, g * gw:(g + 1) * gw] + y
                           + x.astype(F32) * d_ref[:, g * gw:(g + 1) * gw])
        else:
            o_ref[st, rows, g * gw:(g + 1) * gw] = y

    def finish(c):
        st, rows = c["st"], c["rows"]
        y = jnp.concatenate(c["ys"], axis=1)
        z = z_ref[st, rows, :].astype(F32)
        o_ref[st, rows, :] = (_rms(y * (z * _sigmoid(z))) * g_ref[...]).astype(o_ref.dtype)

    def sub_body(it, carry):
        sub = (subs - 1 - it) if reverse else it
        for st in range(N_STREAMS):
            c = prep(st, sub)
            for g in range(SSM_GROUPS):
                group(c, g)
            if reverse:
                finish(c)
        return carry

    lax.fori_loop(0, subs, sub_body, 0)


def _streams(a):
    return a.reshape(N_STREAMS, a.shape[0] // N_STREAMS, a.shape[1])


def _stream_spec(nt, reverse):
    def tile_spec(width, col=0):
        if reverse:
            return pl.BlockSpec((N_STREAMS, TOKEN_TILE, width), lambda i: (0, nt - 1 - i, col))
        return pl.BlockSpec((N_STREAMS, TOKEN_TILE, width), lambda i: (0, i, col))
    return tile_spec


def _ssd(xc, bcc, small, dtb_row, a_row, geom, *, reverse, y_fwd=None, proj=None, d_row=None, g_row=None):
    half = xc.shape[1]
    nt = half // TOKEN_TILE
    w = BRANCH_WIDTH
    tile_spec = _stream_spec(nt, reverse)
    in_specs = [tile_spec(w), tile_spec(BC_W), tile_spec(LANES), _resident((1, LANES)), _resident((1, LANES))]
    args = [xc, bcc, small, dtb_row, a_row]
    if reverse:
        in_specs += [tile_spec(w), tile_spec(w, PROJ_BLOCKS["ssm_z"]), _resident((1, w)), _resident((1, w))]
        args += [y_fwd, proj, d_row, g_row]
    return pl.pallas_call(
        functools.partial(_ssd_body, geom=geom, reverse=reverse, n_tiles=nt, half_tokens=half),
        out_shape=jax.ShapeDtypeStruct((N_STREAMS, half, w), BF16 if reverse else F32), grid=(nt,),
        in_specs=in_specs, out_specs=tile_spec(w),
        scratch_shapes=[pltpu.VMEM((N_STREAMS, SSM_GROUPS, SSM_STATE, w // SSM_GROUPS), F32)],
        compiler_params=_cparams(("arbitrary",)), name="ssd_bwd" if reverse else "ssd_fwd")(*args)


def _mlstm_body(*refs, geom, reverse, n_tiles, half_tokens):
    if reverse:
        q_ref, k_ref, v_ref, sm_ref, gb_ref, hf_ref, og_ref, g_ref, o_ref, cn_ref, m_ref = refs
    else:
        q_ref, k_ref, v_ref, sm_ref, gb_ref, o_ref, cn_ref, m_ref = refs
    step = pl.program_id(0)
    tile = (n_tiles - 1 - step) if reverse else step
    mask, tri = _chunk_masks(reverse)
    ig_off = SMALL_IG + (ML_HEADS if reverse else 0)
    fg_off = SMALL_FG + (ML_HEADS if reverse else 0)
    subs = TOKEN_TILE // CHUNK
    k_scale = ML_HEAD_DIM ** -0.5

    def prep(st, sub):
        local, seg = _seg_pos(st * half_tokens + tile * TOKEN_TILE + sub * CHUNK, geom)
        rows = pl.ds(pl.multiple_of(sub * CHUNK, CHUNK), CHUNK)
        gates = sm_ref[st, rows, :] + gb_ref[...]
        fcs = _cumsum_rows(tri, -_softplus(-gates))
        return dict(st=st, rows=rows, start=(local + CHUNK == seg) if reverse else (local == 0),
                    gates=gates, fcs=fcs, fcs_row=fcs.T, gates_row=gates.T,
                    ftot=fcs[0:1, :] if reverse else fcs[CHUNK - 1:CHUNK, :])

    def head(c, h):
        st, rows, start = c["st"], c["rows"], c["start"]
        ci, cf = ig_off + h, fg_off + h
        hsl = slice(h * ML_HEAD_DIM, (h + 1) * ML_HEAD_DIM)
        d = ML_HEAD_DIM
        fc = c["fcs"][:, cf:cf + 1]
        fcs_r = c["fcs_row"][cf:cf + 1, :]
        ig_r = c["gates_row"][ci:ci + 1, :]
        m_st = jnp.where(start, 0.0, m_ref[st, h:h + 1, :])
        r = jnp.where(mask, ig_r - fcs_r, -jnp.inf)
        g = jnp.maximum(jnp.max(r, axis=-1, keepdims=True), m_st)
        sc = jnp.exp(m_st - g)
        qh = q_ref[st, rows, hsl]
        kh = k_ref[st, rows, hsl]
        v1 = jnp.concatenate([v_ref[st, rows, hsl], jnp.ones((CHUNK, d), BF16)], axis=1)
        s = (_dot_nt(qh, kh) * k_scale) * jnp.exp(r - g)
        cn = jnp.where(start, 0.0, cn_ref[st, h])
        both = _dot(s.astype(BF16), v1) + jnp.concatenate([sc, sc], axis=1) * _dot(qh, cn.astype(BF16))
        hh = both[:, :d] / jnp.maximum(jnp.abs(both[:, d:]), jnp.exp(-(fc + g)))
        f_last = jnp.broadcast_to(c["ftot"][:, cf:cf + 1], (1, LANES))
        logw = f_last - fcs_r + ig_r
        m_new = jnp.maximum(f_last + m_st, jnp.max(logw, axis=-1, keepdims=True))
        kw_t = kh.astype(F32).T * (jnp.exp(logw - m_new) * k_scale)
        dec = jnp.exp(f_last + m_st - m_new)
        cn_ref[st, h] = jnp.concatenate([dec, dec], axis=1) * cn + _dot(kw_t.astype(BF16), v1)
        m_ref[st, h:h + 1, :] = m_new
        if reverse:
            hh = (_rms(hf_ref[st, rows, hsl] + hh) * g_ref[:, hsl]
                  * _sigmoid(og_ref[st, rows, hsl].astype(F32)))
        o_ref[st, rows, hsl] = hh.astype(o_ref.dtype)

    def sub_body(it, carry):
        sub = (subs - 1 - it) if reverse else it
        for st in range(N_STREAMS):
            c = prep(st, sub)
            for h in range(ML_HEADS):
                head(c, h)
        return carry

    lax.fori_loop(0, subs, sub_body, 0)


def _mlstm(proj, small, gate_bias_row, geom, *, reverse, h_fwd=None, g_row=None):
    half = proj.shape[1]
    nt = half // TOKEN_TILE
    w = BRANCH_WIDTH
    tile_spec = _stream_spec(nt, reverse)
    in_specs = [tile_spec(w, PROJ_BLOCKS["ml_q"]), tile_spec(w, PROJ_BLOCKS["ml_k"]),
                tile_spec(w, PROJ_BLOCKS["ml_v"]), tile_spec(LANES), _resident((1, LANES))]
    args = [proj, proj, proj, small, gate_bias_row]
    if reverse:
        in_specs += [tile_spec(w), tile_spec(w, PROJ_BLOCKS["ml_o"]), _resident((1, w))]
        args += [h_fwd, proj, g_row]
    return pl.pallas_call(
        functools.partial(_mlstm_body, geom=geom, reverse=reverse, n_tiles=nt, half_tokens=half),
        out_shape=jax.ShapeDtypeStruct((N_STREAMS, half, w), BF16 if reverse else F32), grid=(nt,),
        in_specs=in_specs, out_specs=tile_spec(w),
        scratch_shapes=[pltpu.VMEM((N_STREAMS, ML_HEADS, ML_HEAD_DIM, 2 * ML_HEAD_DIM), F32),
                        pltpu.VMEM((N_STREAMS, SUBLANES, LANES), F32)],
        compiler_params=_cparams(("arbitrary",)), name="mlstm_bwd" if reverse else "mlstm_fwd")(*args)


def _sgu_body(u_ref, v_ref, g_ref, w_ref, b_ref, o_ref):
    u = _gelu_tanh(u_ref[...].astype(F32))
    vg = (_rms(_gelu_tanh(v_ref[...].astype(F32))) * g_ref[...]).astype(BF16)
    for c in range(TOKEN_TILE // CHUNK):
        rows = slice(c * CHUNK, (c + 1) * CHUNK)
        outs = []
        for g in range(SGU_GROUPS):
            cols = slice(g * SGU_GROUP_DIM, (g + 1) * SGU_GROUP_DIM)
            outs.append(_dot(w_ref[g], vg[rows, cols]) + b_ref[:, g:g + 1])
        o_ref[rows, :] = (u[rows, :] * jnp.concatenate(outs, axis=1)).astype(o_ref.dtype)


def _sgu(proj, g_row, w, b_cols):
    n = proj.shape[0]
    wd = BRANCH_WIDTH
    return pl.pallas_call(
        _sgu_body, out_shape=jax.ShapeDtypeStruct((n, wd), BF16), grid=(n // TOKEN_TILE,),
        in_specs=[pl.BlockSpec((TOKEN_TILE, wd), lambda i: (i, PROJ_BLOCKS["sgu_u"])),
                  pl.BlockSpec((TOKEN_TILE, wd), lambda i: (i, PROJ_BLOCKS["sgu_v"])),
                  _resident((1, wd)), _resident(w.shape), _resident(b_cols.shape)],
        out_specs=pl.BlockSpec((TOKEN_TILE, wd), lambda i: (i, 0)),
        compiler_params=_cparams(("parallel",)), name="sgu")(proj, proj, g_row, w, b_cols)


def _merge_body(x_ref, gm_ref, wg_ref, bg_ref, y0_ref, y1_ref, y2_ref, y3_ref, wb_ref, wo_ref, o_ref):
    x = x_ref[...]
    h = (_rms(x) * gm_ref[...]).astype(BF16)
    merged = None
    for i, y_ref in enumerate((y0_ref, y1_ref, y2_ref, y3_ref)):
        cols = slice(i * D_MODEL, (i + 1) * D_MODEL)
        term = _sigmoid(_dot(h, wg_ref[:, cols]) + bg_ref[:, cols]) * _dot(y_ref[...], wb_ref[i])
        merged = term if merged is None else merged + term
    o_ref[...] = x + _dot(merged.astype(BF16), wo_ref[...])


def _merge(x, gm_row, wg, bg_row, branches, wb, wo):
    n, d = x.shape
    br_spec = pl.BlockSpec((TOKEN_TILE, BRANCH_WIDTH), lambda i: (i, 0))
    x_spec = pl.BlockSpec((TOKEN_TILE, d), lambda i: (i, 0))
    return pl.pallas_call(
        _merge_body, out_shape=jax.ShapeDtypeStruct((n, d), F32), grid=(n // TOKEN_TILE,),
        in_specs=[x_spec, _resident((1, d)), _resident(wg.shape), _resident(bg_row.shape),
                  br_spec, br_spec, br_spec, br_spec, _resident(wb.shape), _resident(wo.shape)],
        out_specs=x_spec, compiler_params=_cparams(("parallel",)), name="merge")(
            x, gm_row, wg, bg_row, *branches, wb, wo)


def _ffn_body(x_ref, xp_ref, xn_ref, g_ref, wu_ref, cw_ref, cb_ref, wd_ref, gf_ref, o_ref, act_ref, sa_ref,
              sb_ref, *, geom, final):
    first, last = _tile_ends(geom)
    t = TOKEN_TILE
    hp = SUBLANES
    x = x_ref[...]
    h = (_rms(jnp.concatenate([xp_ref[...], x, xn_ref[...]], axis=0)) * g_ref[...]).astype(BF16)
    pad = FFN_CONV_W // 2

    def project(s_ref, slot, cols):
        up = _dot(h, wu_ref[:, cols])
        s_ref[slot, 0:hp, :] = jnp.where(first, 0.0, up[0:hp, :])
        s_ref[slot, hp:hp + t, :] = up[hp:hp + t, :]
        s_ref[slot, hp + t:, :] = jnp.where(last, 0.0, up[hp + t:, :])

    def project_chunk(c):
        c0 = c * FFN_COL_CHUNK
        project(sa_ref, c % 2, slice(c0, c0 + FFN_COL_CHUNK))
        project(sb_ref, c % 2, slice(D_FF + c0, D_FF + c0 + FFN_COL_CHUNK))

    def conv3(s_ref, slot, cols):
        acc = cb_ref[:, cols]
        for k in range(FFN_CONV_W):
            acc = acc + cw_ref[k:k + 1, cols] * s_ref[slot, hp + k - pad:hp + k - pad + t, :]
        return acc

    o_ref[...] = x
    n_chunks = D_FF // FFN_COL_CHUNK
    per_group = FFN_DOWN_GROUP // FFN_COL_CHUNK
    project_chunk(0)
    for c in range(n_chunks):
        if c + 1 < n_chunks:
            project_chunk(c + 1)
        c0 = c * FFN_COL_CHUNK
        a = conv3(sa_ref, c % 2, slice(c0, c0 + FFN_COL_CHUNK))
        b = conv3(sb_ref, c % 2, slice(D_FF + c0, D_FF + c0 + FFN_COL_CHUNK))
        k0 = (c % per_group) * FFN_COL_CHUNK
        grp = c // per_group
        act_ref[grp % 2, :, k0:k0 + FFN_COL_CHUNK] = (_gelu_tanh(a) * b).astype(BF16)
        if c % per_group == per_group - 1 or c == n_chunks - 1:
            g0 = grp * FFN_DOWN_GROUP
            o_ref[...] += _dot(act_ref[grp % 2, :, :k0 + FFN_COL_CHUNK], wd_ref[g0:c0 + FFN_COL_CHUNK, :])
    if final:
        o_ref[...] = _rms(o_ref[...]) * gf_ref[...]


def _ffn(x, g_row, w_up, conv_w, conv_b, w_down, gf_row, geom, *, final):
    assert conv_w.shape[0] == FFN_CONV_W
    n, d = x.shape
    per = TOKEN_TILE // SUBLANES
    nh = n // SUBLANES
    x_spec = pl.BlockSpec((TOKEN_TILE, d), lambda i: (i, 0))
    halo_rows = TOKEN_TILE + 2 * SUBLANES
    return pl.pallas_call(
        functools.partial(_ffn_body, geom=geom, final=final),
        out_shape=jax.ShapeDtypeStruct((n, d), F32), grid=(n // TOKEN_TILE,),
        in_specs=[x_spec,
                  pl.BlockSpec((SUBLANES, d), lambda i: (jnp.maximum(i * per - 1, 0), 0)),
                  pl.BlockSpec((SUBLANES, d), lambda i: (jnp.minimum((i + 1) * per, nh - 1), 0)),
                  _resident((1, d)), _resident(w_up.shape), _resident(conv_w.shape),
                  _resident((1, 2 * D_FF)), _resident(w_down.shape), _resident((1, d))],
        out_specs=x_spec,
        scratch_shapes=[pltpu.VMEM((2, TOKEN_TILE, FFN_DOWN_GROUP), BF16),
                        pltpu.VMEM((2, halo_rows, FFN_COL_CHUNK), F32),
                        pltpu.VMEM((2, halo_rows, FFN_COL_CHUNK), F32)],
        compiler_params=_cparams(("parallel",)), name="ffn")(
            x, x, x, g_row, w_up, conv_w, conv_b.reshape(1, 2 * D_FF), w_down, gf_row)


def _pad_row(vec, offset):
    return jnp.zeros((1, LANES), F32).at[0, offset:offset + vec.shape[0]].set(vec.astype(F32))


def kernel(x_prompt, x_sample, norm_mix_g, w_in, na_rpb, ssm_conv_w, ssm_conv_b, ssm_dt_bias, ssm_a_log, ssm_d, ssm_norm_g, ml_igate_b, ml_fgate_b, ml_norm_g, sgu_norm_g, sgu_w, sgu_b, w_branch, w_gate, b_gate, w_out, norm_ffn_g, ffn_w_up, ffn_conv_w, ffn_conv_b, ffn_w_down, norm_final_g):
    bp, lp, d = x_prompt.shape
    bs, ls, _ = x_sample.shape
    assert d == D_MODEL and lp % TOKEN_TILE == 0 and ls % TOKEN_TILE == 0
    assert min(lp, ls) >= 2 * TOKEN_TILE
    geom = (bp * lp, lp, ls)
    n = bp * lp + bs * ls
    for st in range(1, N_STREAMS):
        split = st * n // N_STREAMS
        assert n % N_STREAMS == 0 and (split % lp == 0 if split <= bp * lp else (split - bp * lp) % ls == 0)
    depth = w_in.shape[0]
    x = jnp.concatenate([x_prompt.reshape(bp * lp, d), x_sample.reshape(bs * ls, d)], axis=0)

    o_ssm, o_ml, o_sgu = NA_COLS, NA_COLS + SSM_COLS, NA_COLS + SSM_COLS + ML_COLS
    bc_lo = o_ssm + 2 * BRANCH_WIDTH
    dt_lo = bc_lo + BC_W
    mlg_lo = o_ml + 4 * BRANCH_WIDTH
    for l in range(depth):
        wl = w_in[l]
        w_main = jnp.concatenate([wl[:, :bc_lo], wl[:, o_ml:mlg_lo], wl[:, o_sgu:], wl[:, bc_lo:dt_lo]],
                                 axis=1).astype(BF16)
        w_small = jnp.concatenate([wl[:, dt_lo:o_ml], wl[:, mlg_lo:o_sgu],
                                   jnp.zeros((d, LANES - 2 * SSM_HEADS - 4 * ML_HEADS), F32)],
                                  axis=1).astype(BF16)
        proj, small = _norm_matmul(x, norm_mix_g[l], w_main, w_small, name="in_proj")

        y_na = _neighbourhood_attention(proj, _na_bias_table(na_rpb[l]), geom)

        xc = _conv_silu(proj, PROJ_BLOCKS["ssm_x"], BRANCH_WIDTH, ssm_conv_w[l][:, :BRANCH_WIDTH],
                        ssm_conv_b[l][:BRANCH_WIDTH], geom, name="ssm_conv_x")
        bcc = _conv_silu(proj, BC_BLOCK, BC_W, ssm_conv_w[l][:, BRANCH_WIDTH:], ssm_conv_b[l][BRANCH_WIDTH:],
                         geom, name="ssm_conv_bc")
        dtb_row = _pad_row(ssm_dt_bias[l].reshape(-1), SMALL_DT)
        a_neg = -jnp.exp(ssm_a_log[l].astype(F32))
        proj_s, small_s, xc_s, bcc_s = _streams(proj), _streams(small), _streams(xc), _streams(bcc)
        y_f = _ssd(xc_s, bcc_s, small_s, dtb_row, _pad_row(a_neg[0], SMALL_DT), geom, reverse=False)
        d_row = jnp.repeat(ssm_d[l].astype(F32), SSM_HEAD_DIM).reshape(1, BRANCH_WIDTH)
        y_ssm = _ssd(xc_s, bcc_s, small_s, dtb_row, _pad_row(a_neg[1], SMALL_DT + SSM_HEADS), geom, reverse=True,
                     y_fwd=y_f, proj=proj_s, d_row=d_row, g_row=ssm_norm_g[l].reshape(1, BRANCH_WIDTH))
        y_ssm = y_ssm.reshape(n, BRANCH_WIDTH)

        gate_bias = _pad_row(jnp.concatenate([ml_igate_b[l].reshape(-1), ml_fgate_b[l].reshape(-1)]), SMALL_IG)
        h_f = _mlstm(proj_s, small_s, gate_bias, geom, reverse=False)
        y_ml = _mlstm(proj_s, small_s, gate_bias, geom, reverse=True, h_fwd=h_f,
                      g_row=ml_norm_g[l].reshape(1, BRANCH_WIDTH)).reshape(n, BRANCH_WIDTH)

        sgu_b_cols = jnp.zeros((CHUNK, LANES), F32).at[:, :SGU_GROUPS].set(sgu_b[l].T.astype(F32))
        y_sgu = _sgu(proj, sgu_norm_g[l].reshape(1, BRANCH_WIDTH), sgu_w[l].astype(BF16), sgu_b_cols)

        x = _merge(x, norm_mix_g[l].reshape(1, d), w_gate[l].astype(BF16), b_gate[l].reshape(1, N_BRANCH * d),
                   (y_na, y_ssm, y_ml, y_sgu), w_branch[l].astype(BF16), w_out[l].astype(BF16))

        x = _ffn(x, norm_ffn_g[l].reshape(1, d), ffn_w_up[l].astype(BF16), ffn_conv_w[l], ffn_conv_b[l],
                 ffn_w_down[l].astype(BF16), norm_final_g.reshape(1, d), geom, final=(l == depth - 1))

    return (x[:bp * lp].reshape(bp, lp, d), x[bp * lp:].reshape(bs, ls, d))
```

```python
import functools
import math

import jax
import jax.numpy as jnp
import numpy as np
from jax import lax
from jax.experimental import pallas as pl
from jax.experimental.pallas import tpu as pltpu

F32 = jnp.float32
BF16 = jnp.bfloat16

D_MODEL = 1024
GRID_W = 64
BRANCH_WIDTH = 512
N_BRANCH = 4
NA_HEADS = 8
NA_HEAD_DIM = BRANCH_WIDTH // NA_HEADS
NA_WR = 8
NA_WC = 16
SSM_HEADS = 8
SSM_HEAD_DIM = BRANCH_WIDTH // SSM_HEADS
SSM_GROUPS = 2
SSM_STATE = 64
SSM_CONV_W = 5
CHUNK = 128
XBC_W = BRANCH_WIDTH + 2 * SSM_GROUPS * SSM_STATE
ML_HEADS = 4
ML_HEAD_DIM = BRANCH_WIDTH // ML_HEADS
SGU_GROUPS = 4
SGU_GROUP_DIM = BRANCH_WIDTH // SGU_GROUPS
D_FF = 2816
FFN_CONV_W = 3
NORM_EPS = 1e-6
NA_COLS = 3 * BRANCH_WIDTH
SSM_COLS = BRANCH_WIDTH + XBC_W + 2 * SSM_HEADS
ML_COLS = 4 * BRANCH_WIDTH + 4 * ML_HEADS

LANES = 128
SUBLANES = 8
BF16_ROWS = 16
VMEM_LIMIT = 56 * 1024 * 1024

TOKEN_TILE = 512
N_STREAMS = 2
NA_BLOCK_ROWS = TOKEN_TILE // GRID_W
FFN_COL_CHUNK = 256
FFN_DOWN_GROUP = 1024
MASK_VALUE = -1e30

PROJ_BLOCKS = dict(na_q=0, na_k=1, na_v=2, ssm_z=3, ssm_x=4, ml_q=5, ml_k=6, ml_v=7, ml_o=8,
                   sgu_u=9, sgu_v=10)
PROJ_MAIN_W = 11 * BRANCH_WIDTH + 2 * SSM_GROUPS * SSM_STATE
BC_W = 2 * SSM_GROUPS * SSM_STATE
BC_BLOCK = (11 * BRANCH_WIDTH) // BC_W
SMALL_DT = 0
SMALL_IG = 2 * SSM_HEADS
SMALL_FG = SMALL_IG + 2 * ML_HEADS


def _cparams(sem):
    return pltpu.CompilerParams(dimension_semantics=sem, vmem_limit_bytes=VMEM_LIMIT)


def _resident(shape):
    nd = len(shape)
    return pl.BlockSpec(shape, lambda *_: (0,) * nd, pipeline_mode=pl.Buffered(1))


def _seg_pos(pos, geom):
    n_prompt, len_prompt, len_sample = geom
    in_prompt = pos < n_prompt
    local = jnp.where(in_prompt, pos % len_prompt, (pos - n_prompt) % len_sample)
    return local, jnp.where(in_prompt, len_prompt, len_sample)


def _rms(x):
    return x * lax.rsqrt(jnp.mean(x * x, axis=-1, keepdims=True) + NORM_EPS)


def _sigmoid(x):
    return 1.0 / (1.0 + jnp.exp(-x))


def _softplus(x):
    return jnp.maximum(x, 0.0) + jnp.log(1.0 + jnp.exp(-jnp.abs(x)))


def _gelu_tanh(x):
    return 0.5 * x * (1.0 + jnp.tanh(math.sqrt(2.0 / math.pi) * (x + 0.044715 * (x * x * x))))


def _dot(a, b):
    return jnp.dot(a, b, preferred_element_type=F32)


def _dot_nt(a, b):
    return lax.dot_general(a, b, (((1,), (1,)), ((), ())), preferred_element_type=F32)


def _cumsum_rows(tri, x):
    return jnp.dot(tri, x, preferred_element_type=F32, precision=lax.Precision.HIGHEST)


def _chunk_masks(reverse):
    t = lax.broadcasted_iota(jnp.int32, (CHUNK, CHUNK), 0)
    s = lax.broadcasted_iota(jnp.int32, (CHUNK, CHUNK), 1)
    mask = (s >= t) if reverse else (s <= t)
    return mask, jnp.where(mask, 1.0, 0.0).astype(F32)


def _norm_matmul_body(*refs, chunks, split):
    if split is None:
        x_ref, g_ref, w_ref, ws_ref, o_ref, os_ref = refs
        x = x_ref[...]
    else:
        xa_ref, xb_ref, g_ref, w_ref, ws_ref, o_ref, os_ref, xo_ref = refs
        x = jnp.where(pl.program_id(0) < split, xa_ref[...], xb_ref[...])
        xo_ref[...] = x
    h = (_rms(x) * g_ref[...]).astype(BF16)
    for c0, cw in chunks:
        o_ref[:, c0:c0 + cw] = _dot(h, w_ref[:, c0:c0 + cw]).astype(o_ref.dtype)
    os_ref[...] = _dot(h, ws_ref[...])


def _col_chunks(width, step=512):
    out, c = [], 0
    while c < width:
        out.append((c, min(step, width - c)))
        c += step
    return tuple(out)


def _in_proj(xs, g, w, w_small):
    d = w.shape[0]
    cols, cols_small = w.shape[1], w_small.shape[1]
    n = sum(a.shape[0] for a in xs)
    row = lambda width: pl.BlockSpec((TOKEN_TILE, width), lambda i: (i, 0))
    if len(xs) == 1:
        split = None
        x_specs = [row(d)]
    else:
        split = xs[0].shape[0] // TOKEN_TILE
        last_b = xs[1].shape[0] // TOKEN_TILE - 1
        x_specs = [pl.BlockSpec((TOKEN_TILE, d), lambda i: (jnp.minimum(i, split - 1), 0)),
                   pl.BlockSpec((TOKEN_TILE, d), lambda i: (jnp.clip(i - split, 0, last_b), 0))]
    out_shape = [jax.ShapeDtypeStruct((n, cols), BF16), jax.ShapeDtypeStruct((n, cols_small), F32)]
    out_specs = [row(cols), row(cols_small)]
    if split is not None:
        out_shape.append(jax.ShapeDtypeStruct((n, d), F32))
        out_specs.append(row(d))
    return pl.pallas_call(
        functools.partial(_norm_matmul_body, chunks=_col_chunks(cols), split=split),
        out_shape=out_shape, grid=(n // TOKEN_TILE,),
        in_specs=x_specs + [_resident((1, d)), _resident(w.shape), _resident(w_small.shape)],
        out_specs=out_specs, compiler_params=_cparams(("parallel",)), name="in_proj")(
            *xs, g.reshape(1, d), w, w_small)


def _na_bias_table(rpb):
    cols = np.arange(GRID_W)
    col_start = np.clip(cols - NA_WC // 2, 0, GRID_W - NA_WC)
    valid = (cols[None, :] >= col_start[:, None]) & (cols[None, :] < col_start[:, None] + NA_WC)
    sel_r = np.zeros((2 * NA_WR - 1, NA_WR, NA_WR), np.float32)
    off, i = np.meshgrid(np.arange(NA_WR), np.arange(NA_WR), indexing="ij")
    sel_r[i - off + (NA_WR - 1), off, i] = 1.0
    sel_c = np.zeros((2 * NA_WC - 1, GRID_W, GRID_W), np.float32)
    q, kc = np.nonzero(valid)
    sel_c[kc - q + (NA_WC - 1), q, kc] = 1.0
    t = jnp.einsum("hrc,roi,cqk->ohqik", rpb.astype(F32), sel_r, sel_c, precision=lax.Precision.HIGHEST)
    t = jnp.where(jnp.asarray(valid)[None, None, :, None, :], t, MASK_VALUE)
    return t.reshape(NA_WR, NA_HEADS * GRID_W, NA_WR * GRID_W)


def _na_body(q_ref, kp_ref, kc_ref, kn_ref, vp_ref, vc_ref, vn_ref, bias_ref, o_ref, kbuf, vbuf, s_ref, p_ref,
             *, geom):
    local, seg = _seg_pos(pl.program_id(0) * TOKEN_TILE, geom)
    first = local == 0
    last = local + TOKEN_TILE == seg
    t = TOKEN_TILE
    kbuf[0:t, :] = kp_ref[...]
    kbuf[t:2 * t, :] = kc_ref[...]
    kbuf[2 * t:3 * t, :] = kn_ref[...]
    vbuf[0:t, :] = vp_ref[...]
    vbuf[t:2 * t, :] = vc_ref[...]
    vbuf[2 * t:3 * t, :] = vn_ref[...]
    lo = jnp.where(first, 0, -NA_WR)
    hi = jnp.where(last, 0, NA_WR)
    win = NA_WR * GRID_W
    low = lax.broadcasted_iota(jnp.int32, (GRID_W, LANES), 1) < NA_HEAD_DIM
    pair_rows = 2 * GRID_W

    def row_body(j, carry):
        start = NA_BLOCK_ROWS + jnp.clip(j - NA_WR // 2, lo, hi)
        off = NA_BLOCK_ROWS + j - start
        r0 = pl.multiple_of(start * GRID_W, GRID_W)
        q0 = pl.multiple_of(j * GRID_W, GRID_W)
        qrow = q_ref[pl.ds(q0, GRID_W), :].astype(F32) * NA_HEAD_DIM ** -0.5
        for hp in range(NA_HEADS // 2):
            cols = slice(hp * LANES, (hp + 1) * LANES)
            rows = slice(hp * pair_rows, (hp + 1) * pair_rows)
            qp = qrow[:, cols]
            q2 = jnp.concatenate([jnp.where(low, qp, 0.0), jnp.where(low, 0.0, qp)], axis=0).astype(BF16)
            s_ref[rows, :] = _dot_nt(q2, kbuf[pl.ds(r0, win), cols]) + bias_ref[off, rows, :]
        s = s_ref[...]
        p = jnp.exp(s - jnp.max(s, axis=-1, keepdims=True))
        inv = 1.0 / jnp.sum(p, axis=-1, keepdims=True)
        p_ref[...] = p.astype(BF16)
        outs = []
        for hp in range(NA_HEADS // 2):
            cols = slice(hp * LANES, (hp + 1) * LANES)
            rows = slice(hp * pair_rows, (hp + 1) * pair_rows)
            o2 = _dot(p_ref[rows, :], vbuf[pl.ds(r0, win), cols]) * inv[rows, :]
            outs.append(jnp.where(low, o2[:GRID_W, :], o2[GRID_W:, :]))
        o_ref[pl.ds(q0, GRID_W), :] = jnp.concatenate(outs, axis=1).astype(o_ref.dtype)
        return carry

    lax.fori_loop(0, NA_BLOCK_ROWS, row_body, 0)


def _neighbourhood_attention(proj, bias_table, geom):
    n = proj.shape[0]
    nb = n // TOKEN_TILE
    w = BRANCH_WIDTH

    def spec(col, shift):
        return pl.BlockSpec((TOKEN_TILE, w), lambda i: (jnp.clip(i + shift, 0, nb - 1), col))

    kq, kk, kv = PROJ_BLOCKS["na_q"], PROJ_BLOCKS["na_k"], PROJ_BLOCKS["na_v"]
    return pl.pallas_call(
        functools.partial(_na_body, geom=geom),
        out_shape=jax.ShapeDtypeStruct((n, w), BF16), grid=(nb,),
        in_specs=[spec(kq, 0), spec(kk, -1), spec(kk, 0), spec(kk, 1), spec(kv, -1), spec(kv, 0), spec(kv, 1),
                  _resident(bias_table.shape)],
        out_specs=pl.BlockSpec((TOKEN_TILE, w), lambda i: (i, 0)),
        scratch_shapes=[pltpu.VMEM((3 * TOKEN_TILE, w), BF16), pltpu.VMEM((3 * TOKEN_TILE, w), BF16),
                        pltpu.VMEM((NA_HEADS * GRID_W, NA_WR * GRID_W), F32),
                        pltpu.VMEM((NA_HEADS * GRID_W, NA_WR * GRID_W), BF16)],
        compiler_params=_cparams(("parallel",)), name="na_attention")(
            proj, proj, proj, proj, proj, proj, proj, bias_table)


def _dwconv_tile(scr_ref, cur, prev, nxt, w, bias, first, last):
    t = cur.shape[0]
    kw = w.shape[0]
    pad = kw // 2
    scr_ref[0:SUBLANES, :] = jnp.where(first, 0.0, prev[BF16_ROWS - SUBLANES:, :])
    scr_ref[SUBLANES:SUBLANES + t, :] = cur
    scr_ref[SUBLANES + t:2 * SUBLANES + t, :] = jnp.where(last, 0.0, nxt[:SUBLANES, :])
    acc = bias
    for k in range(kw):
        r = SUBLANES + k - pad
        acc = acc + w[k:k + 1, :] * scr_ref[r:r + t, :]
    return acc


def _halo_specs(n, width, col):
    per = TOKEN_TILE // BF16_ROWS
    nh = n // BF16_ROWS
    return [pl.BlockSpec((TOKEN_TILE, width), lambda i: (i, col)),
            pl.BlockSpec((BF16_ROWS, width), lambda i: (jnp.maximum(i * per - 1, 0), col)),
            pl.BlockSpec((BF16_ROWS, width), lambda i: (jnp.minimum((i + 1) * per, nh - 1), col))]


def _tile_ends(geom):
    local, seg = _seg_pos(pl.program_id(0) * TOKEN_TILE, geom)
    return local == 0, local + TOKEN_TILE == seg


def _conv_silu_body(c_ref, p_ref, n_ref, w_ref, b_ref, o_ref, scr_ref, *, geom):
    first, last = _tile_ends(geom)
    y = _dwconv_tile(scr_ref, c_ref[...].astype(F32), p_ref[...].astype(F32), n_ref[...].astype(F32),
                     w_ref[...], b_ref[...], first, last)
    o_ref[...] = (y * _sigmoid(y)).astype(o_ref.dtype)


def _conv_silu(proj, col, width, w, b, geom, *, name):
    n = proj.shape[0]
    return pl.pallas_call(
        functools.partial(_conv_silu_body, geom=geom),
        out_shape=jax.ShapeDtypeStruct((n, width), BF16), grid=(n // TOKEN_TILE,),
        in_specs=_halo_specs(n, width, col) + [_resident(w.shape), _resident((1, width))],
        out_specs=pl.BlockSpec((TOKEN_TILE, width), lambda i: (i, 0)),
        scratch_shapes=[pltpu.VMEM((TOKEN_TILE + 2 * SUBLANES, width), F32)],
        compiler_params=_cparams(("parallel",)), name=name)(proj, proj, proj, w, b.reshape(1, width))


def _ssd_body(*refs, geom, reverse, n_tiles, half_tokens):
    if reverse:
        x_ref, bc_ref, sm_ref, dtb_ref, a_ref, yf_ref, z_ref, d_ref, g_ref, o_ref, h_ref = refs
    else:
        x_ref, bc_ref, sm_ref, dtb_ref, a_ref, o_ref, h_ref = refs
    step = pl.program_id(0)
    tile = (n_tiles - 1 - step) if reverse else step
    mask, tri = _chunk_masks(reverse)
    dir_off = SMALL_DT + (SSM_HEADS if reverse else 0)
    hg = SSM_HEADS // SSM_GROUPS
    gw = hg * SSM_HEAD_DIM
    subs = TOKEN_TILE // CHUNK

    def prep(st, sub):
        local, seg = _seg_pos(st * half_tokens + tile * TOKEN_TILE + sub * CHUNK, geom)
        rows = pl.ds(pl.multiple_of(sub * CHUNK, CHUNK), CHUNK)
        bc = bc_ref[st, rows, :].astype(F32)
        dt = _softplus(sm_ref[st, rows, :] + dtb_ref[...])
        cs = _cumsum_rows(tri, dt * a_ref[...])
        return dict(st=st, rows=rows, start=(local + CHUNK == seg) if reverse else (local == 0),
                    bc=bc, bc_t=bc.T, dt=dt, dt_row=dt.T, cs=cs, cs_row=cs.T,
                    tot=cs[0:1, :] if reverse else cs[CHUNK - 1:CHUNK, :], ys=[])

    def group(c, g):
        st, rows, bc, cs, dt = c["st"], c["rows"], c["bc"], c["cs"], c["dt"]
        b_g = bc[:, g * SSM_STATE:(g + 1) * SSM_STATE].astype(BF16)
        c_lo = SSM_GROUPS * SSM_STATE + g * SSM_STATE
        c_g = bc[:, c_lo:c_lo + SSM_STATE].astype(BF16)
        b_gt = c["bc_t"][g * SSM_STATE:(g + 1) * SSM_STATE, :]
        cb = _dot_nt(c_g, b_g)
        x = x_ref[st, rows, g * gw:(g + 1) * gw]
        h_prev = jnp.where(c["start"], 0.0, h_ref[st, g])
        y_off = _dot(c_g, h_prev.astype(BF16))
        low = lax.broadcasted_iota(jnp.int32, (CHUNK, LANES), 1) < SSM_HEAD_DIM
        head_of_lane = lax.broadcasted_iota(jnp.int32, (1, gw), 1) // SSM_HEAD_DIM
        y_pairs, lhs, dec = [], [], jnp.zeros((1, gw), F32)
        for p in range(hg // 2):
            ps = slice(p * LANES, (p + 1) * LANES)
            yd, ea = [], []
            for j in (2 * p, 2 * p + 1):
                ch = dir_off + g * hg + j
                cs_r = c["cs_row"][ch:ch + 1, :]
                dt_r = c["dt_row"][ch:ch + 1, :]
                a_rep = jnp.broadcast_to(cs[:, ch:ch + 1], (CHUNK, CHUNK))
                decay = jnp.exp(jnp.where(mask, a_rep - cs_r, -jnp.inf))
                yd.append(_dot((cb * decay * dt_r).astype(BF16), x[:, ps]))
                ea.append(jnp.exp(a_rep))
                tot_h = jnp.broadcast_to(c["tot"][:, ch:ch + 1], (1, CHUNK))
                lhs.append(b_gt * (jnp.exp(tot_h - cs_r) * dt_r))
                dec = jnp.where(head_of_lane == j, jnp.exp(c["tot"][:, ch:ch + 1]), dec)
            y_pairs.append(jnp.where(low, yd[0], yd[1]) + y_off[:, ps] * jnp.where(low, ea[0], ea[1]))
        upd = _dot(jnp.concatenate(lhs, axis=0).astype(BF16), x)
        new = jnp.zeros((SSM_STATE, gw), F32)
        for j in range(hg):
            new = jnp.where(head_of_lane == j, upd[j * SSM_STATE:(j + 1) * SSM_STATE, :], new)
        h_ref[st, g] = h_prev * dec + new
        y = jnp.concatenate(y_pairs, axis=1)
        if reverse:
            c["ys"].append(yf_ref[st, rows, g * gw:(g + 1) * gw] + y
                           + x.astype(F32) * d_ref[:, g * gw:(g + 1) * gw])
        else:
            o_ref[st, rows, g * gw:(g + 1) * gw] = y

    def finish(c):
        st, rows = c["st"], c["rows"]
        y = jnp.concatenate(c["ys"], axis=1)
        z = z_ref[st, rows, :].astype(F32)
        o_ref[st, rows, :] = (_rms(y * (z * _sigmoid(z))) * g_ref[...]).astype(o_ref.dtype)

    def sub_body(it, carry):
        sub = (subs - 1 - it) if reverse else it
        for st in range(N_STREAMS):
            c = prep(st, sub)
            for g in range(SSM_GROUPS):
                group(c, g)
            if reverse:
                finish(c)
        return carry

    lax.fori_loop(0, subs, sub_body, 0)


def _streams(a):
    return a.reshape(N_STREAMS, a.shape[0] // N_STREAMS, a.shape[1])


def _stream_spec(nt, reverse):
    def tile_spec(width, col=0):
        if reverse:
            return pl.BlockSpec((N_STREAMS, TOKEN_TILE, width), lambda i: (0, nt - 1 - i, col))
        return pl.BlockSpec((N_STREAMS, TOKEN_TILE, width), lambda i: (0, i, col))
    return tile_spec


def _ssd(xc, bcc, small, dtb_row, a_row, geom, *, reverse, y_fwd=None, proj=None, d_row=None, g_row=None):
    half = xc.shape[1]
    nt = half // TOKEN_TILE
    w = BRANCH_WIDTH
    tile_spec = _stream_spec(nt, reverse)
    in_specs = [tile_spec(w), tile_spec(BC_W), tile_spec(LANES), _resident((1, LANES)), _resident((1, LANES))]
    args = [xc, bcc, small, dtb_row, a_row]
    if reverse:
        in_specs += [tile_spec(w), tile_spec(w, PROJ_BLOCKS["ssm_z"]), _resident((1, w)), _resident((1, w))]
        args += [y_fwd, proj, d_row, g_row]
    return pl.pallas_call(
        functools.partial(_ssd_body, geom=geom, reverse=reverse, n_tiles=nt, half_tokens=half),
        out_shape=jax.ShapeDtypeStruct((N_STREAMS, half, w), BF16 if reverse else F32), grid=(nt,),
        in_specs=in_specs, out_specs=tile_spec(w),
        scratch_shapes=[pltpu.VMEM((N_STREAMS, SSM_GROUPS, SSM_STATE, w // SSM_GROUPS), F32)],
        compiler_params=_cparams(("arbitrary",)), name="ssd_bwd" if reverse else "ssd_fwd")(*args)


def _mlstm_body(*refs, geom, reverse, n_tiles, half_tokens):
    if reverse:
        q_ref, k_ref, v_ref, sm_ref, gb_ref, hf_ref, og_ref, g_ref, o_ref, cn_ref, m_ref = refs
    else:
        q_ref, k_ref, v_ref, sm_ref, gb_ref, o_ref, cn_ref, m_ref = refs
    step = pl.program_id(0)
    tile = (n_tiles - 1 - step) if reverse else step
    mask, tri = _chunk_masks(reverse)
    ig_off = SMALL_IG + (ML_HEADS if reverse else 0)
    fg_off = SMALL_FG + (ML_HEADS if reverse else 0)
    subs = TOKEN_TILE // CHUNK
    k_scale = ML_HEAD_DIM ** -0.5

    def prep(st, sub):
        local, seg = _seg_pos(st * half_tokens + tile * TOKEN_TILE + sub * CHUNK, geom)
        rows = pl.ds(pl.multiple_of(sub * CHUNK, CHUNK), CHUNK)
        gates = sm_ref[st, rows, :] + gb_ref[...]
        fcs = _cumsum_rows(tri, -_softplus(-gates))
        return dict(st=st, rows=rows, start=(local + CHUNK == seg) if reverse else (local == 0),
                    gates=gates, fcs=fcs, fcs_row=fcs.T, gates_row=gates.T,
                    ftot=fcs[0:1, :] if reverse else fcs[CHUNK - 1:CHUNK, :])

    def head(c, h):
        st, rows, start = c["st"], c["rows"], c["start"]
        ci, cf = ig_off + h, fg_off + h
        hsl = slice(h * ML_HEAD_DIM, (h + 1) * ML_HEAD_DIM)
        d = ML_HEAD_DIM
        fc = c["fcs"][:, cf:cf + 1]
        fcs_r = c["fcs_row"][cf:cf + 1, :]
        ig_r = c["gates_row"][ci:ci + 1, :]
        m_st = jnp.where(start, 0.0, m_ref[st, h:h + 1, :])
        r = jnp.where(mask, ig_r - fcs_r, -jnp.inf)
        g = jnp.maximum(jnp.max(r, axis=-1, keepdims=True), m_st)
        sc = jnp.exp(m_st - g)
        qh = q_ref[st, rows, hsl]
        kh = k_ref[st, rows, hsl]
        v1 = jnp.concatenate([v_ref[st, rows, hsl], jnp.ones((CHUNK, d), BF16)], axis=1)
        s = (_dot_nt(qh, kh) * k_scale) * jnp.exp(r - g)
        cn = jnp.where(start, 0.0, cn_ref[st, h])
        both = _dot(s.astype(BF16), v1) + jnp.concatenate([sc, sc], axis=1) * _dot(qh, cn.astype(BF16))
        hh = both[:, :d] / jnp.maximum(jnp.abs(both[:, d:]), jnp.exp(-(fc + g)))
        f_last = jnp.broadcast_to(c["ftot"][:, cf:cf + 1], (1, LANES))
        logw = f_last - fcs_r + ig_r
        m_new = jnp.maximum(f_last + m_st, jnp.max(logw, axis=-1, keepdims=True))
        kw_t = kh.astype(F32).T * (jnp.exp(logw - m_new) * k_scale)
        dec = jnp.exp(f_last + m_st - m_new)
        cn_ref[st, h] = jnp.concatenate([dec, dec], axis=1) * cn + _dot(kw_t.astype(BF16), v1)
        m_ref[st, h:h + 1, :] = m_new
        if reverse:
            hh = (_rms(hf_ref[st, rows, hsl] + hh) * g_ref[:, hsl]
                  * _sigmoid(og_ref[st, rows, hsl].astype(F32)))
        o_ref[st, rows, hsl] = hh.astype(o_ref.dtype)

    def sub_body(it, carry):
        sub = (subs - 1 - it) if reverse else it
        for st in range(N_STREAMS):
            c = prep(st, sub)
            for h in range(ML_HEADS):
                head(c, h)
        return carry

    lax.fori_loop(0, subs, sub_body, 0)


def _mlstm(proj, small, gate_bias_row, geom, *, reverse, h_fwd=None, g_row=None):
    half = proj.shape[1]
    nt = half // TOKEN_TILE
    w = BRANCH_WIDTH
    tile_spec = _stream_spec(nt, reverse)
    in_specs = [tile_spec(w, PROJ_BLOCKS["ml_q"]), tile_spec(w, PROJ_BLOCKS["ml_k"]),
                tile_spec(w, PROJ_BLOCKS["ml_v"]), tile_spec(LANES), _resident((1, LANES))]
    args = [proj, proj, proj, small, gate_bias_row]
    if reverse:
        in_specs += [tile_spec(w), tile_spec(w, PROJ_BLOCKS["ml_o"]), _resident((1, w))]
        args += [h_fwd, proj, g_row]
    return pl.pallas_call(
        functools.partial(_mlstm_body, geom=geom, reverse=reverse, n_tiles=nt, half_tokens=half),
        out_shape=jax.ShapeDtypeStruct((N_STREAMS, half, w), BF16 if reverse else F32), grid=(nt,),
        in_specs=in_specs, out_specs=tile_spec(w),
        scratch_shapes=[pltpu.VMEM((N_STREAMS, ML_HEADS, ML_HEAD_DIM, 2 * ML_HEAD_DIM), F32),
                        pltpu.VMEM((N_STREAMS, SUBLANES, LANES), F32)],
        compiler_params=_cparams(("arbitrary",)), name="mlstm_bwd" if reverse else "mlstm_fwd")(*args)


def _sgu_body(u_ref, v_ref, g_ref, w_ref, b_ref, o_ref):
    u = _gelu_tanh(u_ref[...].astype(F32))
    vg = (_rms(_gelu_tanh(v_ref[...].astype(F32))) * g_ref[...]).astype(BF16)
    for c in range(TOKEN_TILE // CHUNK):
        rows = slice(c * CHUNK, (c + 1) * CHUNK)
        outs = []
        for g in range(SGU_GROUPS):
            cols = slice(g * SGU_GROUP_DIM, (g + 1) * SGU_GROUP_DIM)
            outs.append(_dot(w_ref[g], vg[rows, cols]) + b_ref[:, g:g + 1])
        o_ref[rows, :] = (u[rows, :] * jnp.concatenate(outs, axis=1)).astype(o_ref.dtype)


def _sgu(proj, g_row, w, b_cols):
    n = proj.shape[0]
    wd = BRANCH_WIDTH
    return pl.pallas_call(
        _sgu_body, out_shape=jax.ShapeDtypeStruct((n, wd), BF16), grid=(n // TOKEN_TILE,),
        in_specs=[pl.BlockSpec((TOKEN_TILE, wd), lambda i: (i, PROJ_BLOCKS["sgu_u"])),
                  pl.BlockSpec((TOKEN_TILE, wd), lambda i: (i, PROJ_BLOCKS["sgu_v"])),
                  _resident((1, wd)), _resident(w.shape), _resident(b_cols.shape)],
        out_specs=pl.BlockSpec((TOKEN_TILE, wd), lambda i: (i, 0)),
        compiler_params=_cparams(("parallel",)), name="sgu")(proj, proj, g_row, w, b_cols)


def _merge_body(x_ref, gm_ref, wg_ref, bg_ref, y0_ref, y1_ref, y2_ref, y3_ref, wb_ref, wo_ref, o_ref):
    x = x_ref[...]
    h = (_rms(x) * gm_ref[...]).astype(BF16)
    merged = None
    for i, y_ref in enumerate((y0_ref, y1_ref, y2_ref, y3_ref)):
        cols = slice(i * D_MODEL, (i + 1) * D_MODEL)
        term = _sigmoid(_dot(h, wg_ref[:, cols]) + bg_ref[:, cols]) * _dot(y_ref[...], wb_ref[i])
        merged = term if merged is None else merged + term
    o_ref[...] = x + _dot(merged.astype(BF16), wo_ref[...])


def _merge(x, gm_row, wg, bg_row, branches, wb, wo):
    n, d = x.shape
    br_spec = pl.BlockSpec((TOKEN_TILE, BRANCH_WIDTH), lambda i: (i, 0))
    x_spec = pl.BlockSpec((TOKEN_TILE, d), lambda i: (i, 0))
    return pl.pallas_call(
        _merge_body, out_shape=jax.ShapeDtypeStruct((n, d), F32), grid=(n // TOKEN_TILE,),
        in_specs=[x_spec, _resident((1, d)), _resident(wg.shape), _resident(bg_row.shape),
                  br_spec, br_spec, br_spec, br_spec, _resident(wb.shape), _resident(wo.shape)],
        out_specs=x_spec, compiler_params=_cparams(("parallel",)), name="merge")(
            x, gm_row, wg, bg_row, *branches, wb, wo)


def _ffn_body(x_ref, xp_ref, xn_ref, g_ref, wu_ref, cw_ref, cb_ref, wd_ref, gf_ref, o_ref, act_ref, sa_ref,
              sb_ref, *, geom, final):
    first, last = _tile_ends(geom)
    t = TOKEN_TILE
    hp = SUBLANES
    x = x_ref[...]
    h = (_rms(jnp.concatenate([xp_ref[...], x, xn_ref[...]], axis=0)) * g_ref[...]).astype(BF16)
    pad = FFN_CONV_W // 2

    def project(s_ref, slot, cols):
        up = _dot(h, wu_ref[:, cols])
        s_ref[slot, 0:hp, :] = jnp.where(first, 0.0, up[0:hp, :])
        s_ref[slot, hp:hp + t, :] = up[hp:hp + t, :]
        s_ref[slot, hp + t:, :] = jnp.where(last, 0.0, up[hp + t:, :])

    def project_chunk(c):
        c0 = c * FFN_COL_CHUNK
        project(sa_ref, c % 2, slice(c0, c0 + FFN_COL_CHUNK))
        project(sb_ref, c % 2, slice(D_FF + c0, D_FF + c0 + FFN_COL_CHUNK))

    def conv3(s_ref, slot, cols):
        acc = cb_ref[:, cols]
        for k in range(FFN_CONV_W):
            acc = acc + cw_ref[k:k + 1, cols] * s_ref[slot, hp + k - pad:hp + k - pad + t, :]
        return acc

    o_ref[...] = x
    n_chunks = D_FF // FFN_COL_CHUNK
    per_group = FFN_DOWN_GROUP // FFN_COL_CHUNK
    project_chunk(0)
    for c in range(n_chunks):
        if c + 1 < n_chunks:
            project_chunk(c + 1)
        c0 = c * FFN_COL_CHUNK
        a = conv3(sa_ref, c % 2, slice(c0, c0 + FFN_COL_CHUNK))
        b = conv3(sb_ref, c % 2, slice(D_FF + c0, D_FF + c0 + FFN_COL_CHUNK))
        k0 = (c % per_group) * FFN_COL_CHUNK
        grp = c // per_group
        act_ref[grp % 2, :, k0:k0 + FFN_COL_CHUNK] = (_gelu_tanh(a) * b).astype(BF16)
        if c % per_group == per_group - 1 or c == n_chunks - 1:
            g0 = grp * FFN_DOWN_GROUP
            o_ref[...] += _dot(act_ref[grp % 2, :, :k0 + FFN_COL_CHUNK], wd_ref[g0:c0 + FFN_COL_CHUNK, :])
    if final:
        o_ref[...] = _rms(o_ref[...]) * gf_ref[...]


def _ffn(x, g_row, w_up, conv_w, conv_b, w_down, gf_row, geom, *, final):
    assert conv_w.shape[0] == FFN_CONV_W
    n, d = x.shape
    per = TOKEN_TILE // SUBLANES
    nh = n // SUBLANES
    x_spec = pl.BlockSpec((TOKEN_TILE, d), lambda i: (i, 0))
    halo_rows = TOKEN_TILE + 2 * SUBLANES
    return pl.pallas_call(
        functools.partial(_ffn_body, geom=geom, final=final),
        out_shape=jax.ShapeDtypeStruct((n, d), F32), grid=(n // TOKEN_TILE,),
        in_specs=[x_spec,
                  pl.BlockSpec((SUBLANES, d), lambda i: (jnp.maximum(i * per - 1, 0), 0)),
                  pl.BlockSpec((SUBLANES, d), lambda i: (jnp.minimum((i + 1) * per, nh - 1), 0)),
                  _resident((1, d)), _resident(w_up.shape), _resident(conv_w.shape),
                  _resident((1, 2 * D_FF)), _resident(w_down.shape), _resident((1, d))],
        out_specs=x_spec,
        scratch_shapes=[pltpu.VMEM((2, TOKEN_TILE, FFN_DOWN_GROUP), BF16),
                        pltpu.VMEM((2, halo_rows, FFN_COL_CHUNK), F32),
                        pltpu.VMEM((2, halo_rows, FFN_COL_CHUNK), F32)],
        compiler_params=_cparams(("parallel",)), name="ffn")(
            x, x, x, g_row, w_up, conv_w, conv_b.reshape(1, 2 * D_FF), w_down, gf_row)


def _pad_row(vec, offset):
    return jnp.zeros((1, LANES), F32).at[0, offset:offset + vec.shape[0]].set(vec.astype(F32))


def kernel(x_prompt, x_sample, norm_mix_g, w_in, na_rpb, ssm_conv_w, ssm_conv_b, ssm_dt_bias, ssm_a_log, ssm_d, ssm_norm_g, ml_igate_b, ml_fgate_b, ml_norm_g, sgu_norm_g, sgu_w, sgu_b, w_branch, w_gate, b_gate, w_out, norm_ffn_g, ffn_w_up, ffn_conv_w, ffn_conv_b, ffn_w_down, norm_final_g):
    bp, lp, d = x_prompt.shape
    bs, ls, _ = x_sample.shape
    assert d == D_MODEL and lp % TOKEN_TILE == 0 and ls % TOKEN_TILE == 0
    assert min(lp, ls) >= 2 * TOKEN_TILE
    geom = (bp * lp, lp, ls)
    n = bp * lp + bs * ls
    for st in range(1, N_STREAMS):
        split = st * n // N_STREAMS
        assert n % N_STREAMS == 0 and (split % lp == 0 if split <= bp * lp else (split - bp * lp) % ls == 0)
    depth = w_in.shape[0]
    x = None

    o_ssm, o_ml, o_sgu = NA_COLS, NA_COLS + SSM_COLS, NA_COLS + SSM_COLS + ML_COLS
    bc_lo = o_ssm + 2 * BRANCH_WIDTH
    dt_lo = bc_lo + BC_W
    mlg_lo = o_ml + 4 * BRANCH_WIDTH
    for l in range(depth):
        wl = w_in[l]
        w_main = jnp.concatenate([wl[:, :bc_lo], wl[:, o_ml:mlg_lo], wl[:, o_sgu:], wl[:, bc_lo:dt_lo]],
                                 axis=1).astype(BF16)
        w_small = jnp.concatenate([wl[:, dt_lo:o_ml], wl[:, mlg_lo:o_sgu],
                                   jnp.zeros((d, LANES - 2 * SSM_HEADS - 4 * ML_HEADS), F32)],
                                  axis=1).astype(BF16)
        if l == 0:
            proj, small, x = _in_proj((x_prompt.reshape(bp * lp, d), x_sample.reshape(bs * ls, d)),
                                      norm_mix_g[l], w_main, w_small)
        else:
            proj, small = _in_proj((x,), norm_mix_g[l], w_main, w_small)

        y_na = _neighbourhood_attention(proj, _na_bias_table(na_rpb[l]), geom)

        xc = _conv_silu(proj, PROJ_BLOCKS["ssm_x"], BRANCH_WIDTH, ssm_conv_w[l][:, :BRANCH_WIDTH],
                        ssm_conv_b[l][:BRANCH_WIDTH], geom, name="ssm_conv_x")
        bcc = _conv_silu(proj, BC_BLOCK, BC_W, ssm_conv_w[l][:, BRANCH_WIDTH:], ssm_conv_b[l][BRANCH_WIDTH:],
                         geom, name="ssm_conv_bc")
        dtb_row = _pad_row(ssm_dt_bias[l].reshape(-1), SMALL_DT)
        a_neg = -jnp.exp(ssm_a_log[l].astype(F32))
        proj_s, small_s, xc_s, bcc_s = _streams(proj), _streams(small), _streams(xc), _streams(bcc)
        y_f = _ssd(xc_s, bcc_s, small_s, dtb_row, _pad_row(a_neg[0], SMALL_DT), geom, reverse=False)
        d_row = jnp.repeat(ssm_d[l].astype(F32), SSM_HEAD_DIM).reshape(1, BRANCH_WIDTH)
        y_ssm = _ssd(xc_s, bcc_s, small_s, dtb_row, _pad_row(a_neg[1], SMALL_DT + SSM_HEADS), geom, reverse=True,
                     y_fwd=y_f, proj=proj_s, d_row=d_row, g_row=ssm_norm_g[l].reshape(1, BRANCH_WIDTH))
        y_ssm = y_ssm.reshape(n, BRANCH_WIDTH)

        gate_bias = _pad_row(jnp.concatenate([ml_igate_b[l].reshape(-1), ml_fgate_b[l].reshape(-1)]), SMALL_IG)
        h_f = _mlstm(proj_s, small_s, gate_bias, geom, reverse=False)
        y_ml = _mlstm(proj_s, small_s, gate_bias, geom, reverse=True, h_fwd=h_f,
                      g_row=ml_norm_g[l].reshape(1, BRANCH_WIDTH)).reshape(n, BRANCH_WIDTH)

        sgu_b_cols = jnp.zeros((CHUNK, LANES), F32).at[:, :SGU_GROUPS].set(sgu_b[l].T.astype(F32))
        y_sgu = _sgu(proj, sgu_norm_g[l].reshape(1, BRANCH_WIDTH), sgu_w[l].astype(BF16), sgu_b_cols)

        x = _merge(x, norm_mix_g[l].reshape(1, d), w_gate[l].astype(BF16), b_gate[l].reshape(1, N_BRANCH * d),
                   (y_na, y_ssm, y_ml, y_sgu), w_branch[l].astype(BF16), w_out[l].astype(BF16))

        x = _ffn(x, norm_ffn_g[l].reshape(1, d), ffn_w_up[l].astype(BF16), ffn_conv_w[l], ffn_conv_b[l],
                 ffn_w_down[l].astype(BF16), norm_final_g.reshape(1, d), geom, final=(l == depth - 1))

    return (x[:bp * lp].reshape(bp, lp, d), x[bp * lp:].reshape(bs, ls, d))
```

```python
import functools
import math

import jax
import jax.numpy as jnp
import numpy as np
from jax import lax
from jax.experimental import pallas as pl
from jax.experimental.pallas import tpu as pltpu

F32 = jnp.float32
BF16 = jnp.bfloat16

D_MODEL = 1024
GRID_W = 64
BRANCH_WIDTH = 512
N_BRANCH = 4
NA_HEADS = 8
NA_HEAD_DIM = BRANCH_WIDTH // NA_HEADS
NA_WR = 8
NA_WC = 16
SSM_HEADS = 8
SSM_HEAD_DIM = BRANCH_WIDTH // SSM_HEADS
SSM_GROUPS = 2
SSM_STATE = 64
SSM_CONV_W = 5
CHUNK = 128
XBC_W = BRANCH_WIDTH + 2 * SSM_GROUPS * SSM_STATE
ML_HEADS = 4
ML_HEAD_DIM = BRANCH_WIDTH // ML_HEADS
SGU_GROUPS = 4
SGU_GROUP_DIM = BRANCH_WIDTH // SGU_GROUPS
D_FF = 2816
FFN_CONV_W = 3
NORM_EPS = 1e-6
NA_COLS = 3 * BRANCH_WIDTH
SSM_COLS = BRANCH_WIDTH + XBC_W + 2 * SSM_HEADS
ML_COLS = 4 * BRANCH_WIDTH + 4 * ML_HEADS

LANES = 128
SUBLANES = 8
BF16_ROWS = 16
VMEM_LIMIT = 56 * 1024 * 1024

TOKEN_TILE = 512
N_STREAMS = 2
NA_BLOCK_ROWS = TOKEN_TILE // GRID_W
FFN_COL_CHUNK = 256
FFN_DOWN_GROUP = 1024
MASK_VALUE = -1e30

PROJ_BLOCKS = dict(na_q=0, na_k=1, na_v=2, ssm_z=3, ssm_x=4, ml_q=5, ml_k=6, ml_v=7, ml_o=8,
                   sgu_u=9, sgu_v=10)
PROJ_MAIN_W = 11 * BRANCH_WIDTH + 2 * SSM_GROUPS * SSM_STATE
BC_W = 2 * SSM_GROUPS * SSM_STATE
BC_BLOCK = (11 * BRANCH_WIDTH) // BC_W
SMALL_DT = 0
SMALL_IG = 2 * SSM_HEADS
SMALL_FG = SMALL_IG + 2 * ML_HEADS


def _cparams(sem):
    return pltpu.CompilerParams(dimension_semantics=sem, vmem_limit_bytes=VMEM_LIMIT)


def _resident(shape):
    nd = len(shape)
    return pl.BlockSpec(shape, lambda *_: (0,) * nd, pipeline_mode=pl.Buffered(1))


def _seg_pos(pos, geom):
    n_prompt, len_prompt, len_sample = geom
    in_prompt = pos < n_prompt
    local = jnp.where(in_prompt, pos % len_prompt, (pos - n_prompt) % len_sample)
    return local, jnp.where(in_prompt, len_prompt, len_sample)


def _rms(x):
    return x * lax.rsqrt(jnp.mean(x * x, axis=-1, keepdims=True) + NORM_EPS)


def _sigmoid(x):
    return 1.0 / (1.0 + jnp.exp(-x))


def _softplus(x):
    return jnp.maximum(x, 0.0) + jnp.log(1.0 + jnp.exp(-jnp.abs(x)))


def _gelu_tanh(x):
    return 0.5 * x * (1.0 + jnp.tanh(math.sqrt(2.0 / math.pi) * (x + 0.044715 * (x * x * x))))


def _dot(a, b):
    return jnp.dot(a, b, preferred_element_type=F32)


def _dot_nt(a, b):
    return lax.dot_general(a, b, (((1,), (1,)), ((), ())), preferred_element_type=F32)


def _cumsum_rows(tri, x):
    return jnp.dot(tri, x, preferred_element_type=F32, precision=lax.Precision.HIGHEST)


def _chunk_masks(reverse):
    t = lax.broadcasted_iota(jnp.int32, (CHUNK, CHUNK), 0)
    s = lax.broadcasted_iota(jnp.int32, (CHUNK, CHUNK), 1)
    mask = (s >= t) if reverse else (s <= t)
    return mask, jnp.where(mask, 1.0, 0.0).astype(F32)


def _norm_matmul_body(*refs, chunks, split):
    if split is None:
        x_ref, g_ref, w_ref, ws_ref, o_ref, os_ref = refs
        x = x_ref[...]
    else:
        xa_ref, xb_ref, g_ref, w_ref, ws_ref, o_ref, os_ref, xo_ref = refs
        x = jnp.where(pl.program_id(0) < split, xa_ref[...], xb_ref[...])
        xo_ref[...] = x
    h = (_rms(x) * g_ref[...]).astype(BF16)
    for c0, cw in chunks:
        o_ref[:, c0:c0 + cw] = _dot(h, w_ref[:, c0:c0 + cw]).astype(o_ref.dtype)
    os_ref[...] = _dot(h, ws_ref[...])


def _col_chunks(width, step=512):
    out, c = [], 0
    while c < width:
        out.append((c, min(step, width - c)))
        c += step
    return tuple(out)


def _in_proj(xs, g, w, w_small):
    d = w.shape[0]
    cols, cols_small = w.shape[1], w_small.shape[1]
    n = sum(a.shape[0] for a in xs)
    row = lambda width: pl.BlockSpec((TOKEN_TILE, width), lambda i: (i, 0))
    if len(xs) == 1:
        split = None
        x_specs = [row(d)]
    else:
        split = xs[0].shape[0] // TOKEN_TILE
        last_b = xs[1].shape[0] // TOKEN_TILE - 1
        x_specs = [pl.BlockSpec((TOKEN_TILE, d), lambda i: (jnp.minimum(i, split - 1), 0)),
                   pl.BlockSpec((TOKEN_TILE, d), lambda i: (jnp.clip(i - split, 0, last_b), 0))]
    out_shape = [jax.ShapeDtypeStruct((n, cols), BF16), jax.ShapeDtypeStruct((n, cols_small), F32)]
    out_specs = [row(cols), row(cols_small)]
    if split is not None:
        out_shape.append(jax.ShapeDtypeStruct((n, d), F32))
        out_specs.append(row(d))
    return pl.pallas_call(
        functools.partial(_norm_matmul_body, chunks=_col_chunks(cols), split=split),
        out_shape=out_shape, grid=(n // TOKEN_TILE,),
        in_specs=x_specs + [_resident((1, d)), _resident(w.shape), _resident(w_small.shape)],
        out_specs=out_specs, compiler_params=_cparams(("parallel",)), name="in_proj")(
            *xs, g.reshape(1, d), w, w_small)


def _na_bias_table(rpb):
    cols = np.arange(GRID_W)
    col_start = np.clip(cols - NA_WC // 2, 0, GRID_W - NA_WC)
    valid = (cols[None, :] >= col_start[:, None]) & (cols[None, :] < col_start[:, None] + NA_WC)
    sel_r = np.zeros((2 * NA_WR - 1, NA_WR, NA_WR), np.float32)
    off, i = np.meshgrid(np.arange(NA_WR), np.arange(NA_WR), indexing="ij")
    sel_r[i - off + (NA_WR - 1), off, i] = 1.0
    sel_c = np.zeros((2 * NA_WC - 1, GRID_W, GRID_W), np.float32)
    q, kc = np.nonzero(valid)
    sel_c[kc - q + (NA_WC - 1), q, kc] = 1.0
    t = jnp.einsum("hrc,roi,cqk->ohqik", rpb.astype(F32), sel_r, sel_c, precision=lax.Precision.HIGHEST)
    t = jnp.where(jnp.asarray(valid)[None, None, :, None, :], t, MASK_VALUE)
    return t.reshape(NA_WR, NA_HEADS * GRID_W, NA_WR * GRID_W)


def _na_body(q_ref, kp_ref, kc_ref, kn_ref, vp_ref, vc_ref, vn_ref, bias_ref, o_ref, kbuf, vbuf, s_ref, p_ref,
             *, geom):
    local, seg = _seg_pos(pl.program_id(0) * TOKEN_TILE, geom)
    first = local == 0
    last = local + TOKEN_TILE == seg
    t = TOKEN_TILE
    kbuf[0:t, :] = kp_ref[...]
    kbuf[t:2 * t, :] = kc_ref[...]
    kbuf[2 * t:3 * t, :] = kn_ref[...]
    vbuf[0:t, :] = vp_ref[...]
    vbuf[t:2 * t, :] = vc_ref[...]
    vbuf[2 * t:3 * t, :] = vn_ref[...]
    lo = jnp.where(first, 0, -NA_WR)
    hi = jnp.where(last, 0, NA_WR)
    win = NA_WR * GRID_W
    low = lax.broadcasted_iota(jnp.int32, (GRID_W, LANES), 1) < NA_HEAD_DIM
    pair_rows = 2 * GRID_W

    def row_body(j, carry):
        start = NA_BLOCK_ROWS + jnp.clip(j - NA_WR // 2, lo, hi)
        off = NA_BLOCK_ROWS + j - start
        r0 = pl.multiple_of(start * GRID_W, GRID_W)
        q0 = j * GRID_W
        qrow = q_ref[pl.ds(q0, GRID_W), :].astype(F32) * NA_HEAD_DIM ** -0.5
        for hp in range(NA_HEADS // 2):
            cols = slice(hp * LANES, (hp + 1) * LANES)
            rows = slice(hp * pair_rows, (hp + 1) * pair_rows)
            qp = qrow[:, cols]
            q2 = jnp.concatenate([jnp.where(low, qp, 0.0), jnp.where(low, 0.0, qp)], axis=0).astype(BF16)
            s_ref[j % 2, rows, :] = _dot_nt(q2, kbuf[pl.ds(r0, win), cols]) + bias_ref[off, rows, :]
        s = s_ref[j % 2]
        p = jnp.exp(s - jnp.max(s, axis=-1, keepdims=True))
        inv = 1.0 / jnp.sum(p, axis=-1, keepdims=True)
        p_ref[j % 2] = p.astype(BF16)
        outs = []
        for hp in range(NA_HEADS // 2):
            cols = slice(hp * LANES, (hp + 1) * LANES)
            rows = slice(hp * pair_rows, (hp + 1) * pair_rows)
            o2 = _dot(p_ref[j % 2, rows, :], vbuf[pl.ds(r0, win), cols]) * inv[rows, :]
            outs.append(jnp.where(low, o2[:GRID_W, :], o2[GRID_W:, :]))
        o_ref[pl.ds(q0, GRID_W), :] = jnp.concatenate(outs, axis=1).astype(o_ref.dtype)
        return carry

    for j in range(NA_BLOCK_ROWS):
        row_body(j, 0)


def _neighbourhood_attention(proj, bias_table, geom):
    n = proj.shape[0]
    nb = n // TOKEN_TILE
    w = BRANCH_WIDTH

    def spec(col, shift):
        return pl.BlockSpec((TOKEN_TILE, w), lambda i: (jnp.clip(i + shift, 0, nb - 1), col))

    kq, kk, kv = PROJ_BLOCKS["na_q"], PROJ_BLOCKS["na_k"], PROJ_BLOCKS["na_v"]
    return pl.pallas_call(
        functools.partial(_na_body, geom=geom),
        out_shape=jax.ShapeDtypeStruct((n, w), BF16), grid=(nb,),
        in_specs=[spec(kq, 0), spec(kk, -1), spec(kk, 0), spec(kk, 1), spec(kv, -1), spec(kv, 0), spec(kv, 1),
                  _resident(bias_table.shape)],
        out_specs=pl.BlockSpec((TOKEN_TILE, w), lambda i: (i, 0)),
        scratch_shapes=[pltpu.VMEM((3 * TOKEN_TILE, w), BF16), pltpu.VMEM((3 * TOKEN_TILE, w), BF16),
                        pltpu.VMEM((2, NA_HEADS * GRID_W, NA_WR * GRID_W), F32),
                        pltpu.VMEM((2, NA_HEADS * GRID_W, NA_WR * GRID_W), BF16)],
        compiler_params=_cparams(("parallel",)), name="na_attention")(
            proj, proj, proj, proj, proj, proj, proj, bias_table)


def _dwconv_tile(scr_ref, cur, prev, nxt, w, bias, first, last):
    t = cur.shape[0]
    kw = w.shape[0]
    pad = kw // 2
    scr_ref[0:SUBLANES, :] = jnp.where(first, 0.0, prev[BF16_ROWS - SUBLANES:, :])
    scr_ref[SUBLANES:SUBLANES + t, :] = cur
    scr_ref[SUBLANES + t:2 * SUBLANES + t, :] = jnp.where(last, 0.0, nxt[:SUBLANES, :])
    acc = bias
    for k in range(kw):
        r = SUBLANES + k - pad
        acc = acc + w[k:k + 1, :] * scr_ref[r:r + t, :]
    return acc


def _halo_specs(n, width, col):
    per = TOKEN_TILE // BF16_ROWS
    nh = n // BF16_ROWS
    return [pl.BlockSpec((TOKEN_TILE, width), lambda i: (i, col)),
            pl.BlockSpec((BF16_ROWS, width), lambda i: (jnp.maximum(i * per - 1, 0), col)),
            pl.BlockSpec((BF16_ROWS, width), lambda i: (jnp.minimum((i + 1) * per, nh - 1), col))]


def _tile_ends(geom):
    local, seg = _seg_pos(pl.program_id(0) * TOKEN_TILE, geom)
    return local == 0, local + TOKEN_TILE == seg


def _conv_silu_body(c_ref, p_ref, n_ref, w_ref, b_ref, o_ref, scr_ref, *, geom):
    first, last = _tile_ends(geom)
    y = _dwconv_tile(scr_ref, c_ref[...].astype(F32), p_ref[...].astype(F32), n_ref[...].astype(F32),
                     w_ref[...], b_ref[...], first, last)
    o_ref[...] = (y * _sigmoid(y)).astype(o_ref.dtype)


def _conv_silu(proj, col, width, w, b, geom, *, name):
    n = proj.shape[0]
    return pl.pallas_call(
        functools.partial(_conv_silu_body, geom=geom),
        out_shape=jax.ShapeDtypeStruct((n, width), BF16), grid=(n // TOKEN_TILE,),
        in_specs=_halo_specs(n, width, col) + [_resident(w.shape), _resident((1, width))],
        out_specs=pl.BlockSpec((TOKEN_TILE, width), lambda i: (i, 0)),
        scratch_shapes=[pltpu.VMEM((TOKEN_TILE + 2 * SUBLANES, width), F32)],
        compiler_params=_cparams(("parallel",)), name=name)(proj, proj, proj, w, b.reshape(1, width))


def _ssd_body(*refs, geom, reverse, n_tiles, half_tokens):
    if reverse:
        x_ref, bc_ref, sm_ref, dtb_ref, a_ref, yf_ref, z_ref, d_ref, g_ref, o_ref, h_ref = refs
    else:
        x_ref, bc_ref, sm_ref, dtb_ref, a_ref, o_ref, h_ref = refs
    step = pl.program_id(0)
    tile = (n_tiles - 1 - step) if reverse else step
    mask, tri = _chunk_masks(reverse)
    dir_off = SMALL_DT + (SSM_HEADS if reverse else 0)
    hg = SSM_HEADS // SSM_GROUPS
    gw = hg * SSM_HEAD_DIM
    subs = TOKEN_TILE // CHUNK

    def prep(st, sub):
        local, seg = _seg_pos(st * half_tokens + tile * TOKEN_TILE + sub * CHUNK, geom)
        rows = pl.ds(pl.multiple_of(sub * CHUNK, CHUNK), CHUNK)
        bc = bc_ref[st, rows, :].astype(F32)
        dt = _softplus(sm_ref[st, rows, :] + dtb_ref[...])
        cs = _cumsum_rows(tri, dt * a_ref[...])
        return dict(st=st, rows=rows, start=(local + CHUNK == seg) if reverse else (local == 0),
                    bc=bc, bc_t=bc.T, dt=dt, dt_row=dt.T, cs=cs, cs_row=cs.T,
                    tot=cs[0:1, :] if reverse else cs[CHUNK - 1:CHUNK, :], ys=[])

    def group(c, g):
        st, rows, bc, cs, dt = c["st"], c["rows"], c["bc"], c["cs"], c["dt"]
        b_g = bc[:, g * SSM_STATE:(g + 1) * SSM_STATE].astype(BF16)
        c_lo = SSM_GROUPS * SSM_STATE + g * SSM_STATE
        c_g = bc[:, c_lo:c_lo + SSM_STATE].astype(BF16)
        b_gt = c["bc_t"][g * SSM_STATE:(g + 1) * SSM_STATE, :]
        cb = _dot_nt(c_g, b_g)
        x = x_ref[st, rows, g * gw:(g + 1) * gw]
        h_prev = jnp.where(c["start"], 0.0, h_ref[st, g])
        y_off = _dot(c_g, h_prev.astype(BF16))
        low = lax.broadcasted_iota(jnp.int32, (CHUNK, LANES), 1) < SSM_HEAD_DIM
        head_of_lane = lax.broadcasted_iota(jnp.int32, (1, gw), 1) // SSM_HEAD_DIM
        y_pairs, lhs, dec = [], [], jnp.zeros((1, gw), F32)
        for p in range(hg // 2):
            ps = slice(p * LANES, (p + 1) * LANES)
            yd, ea = [], []
            for j in (2 * p, 2 * p + 1):
                ch = dir_off + g * hg + j
                cs_r = c["cs_row"][ch:ch + 1, :]
                dt_r = c["dt_row"][ch:ch + 1, :]
                a_rep = jnp.broadcast_to(cs[:, ch:ch + 1], (CHUNK, CHUNK))
                decay = jnp.exp(jnp.where(mask, a_rep - cs_r, -jnp.inf))
                yd.append(_dot((cb * decay * dt_r).astype(BF16), x[:, ps]))
                ea.append(jnp.exp(a_rep))
                tot_h = jnp.broadcast_to(c["tot"][:, ch:ch + 1], (1, CHUNK))
                lhs.append(b_gt * (jnp.exp(tot_h - cs_r) * dt_r))
                dec = jnp.where(head_of_lane == j, jnp.exp(c["tot"][:, ch:ch + 1]), dec)
            y_pairs.append(jnp.where(low, yd[0], yd[1]) + y_off[:, ps] * jnp.where(low, ea[0], ea[1]))
        upd = _dot(jnp.concatenate(lhs, axis=0).astype(BF16), x)
        new = jnp.zeros((SSM_STATE, gw), F32)
        for j in range(hg):
            new = jnp.where(head_of_lane == j, upd[j * SSM_STATE:(j + 1) * SSM_STATE, :], new)
        h_ref[st, g] = h_prev * dec + new
        y = jnp.concatenate(y_pairs, axis=1)
        if reverse:
            c["ys"].append(yf_ref[st, rows, g * gw:(g + 1) * gw] + y
                           + x.astype(F32) * d_ref[:, g * gw:(g + 1) * gw])
        else:
            o_ref[st, rows, g * gw:(g + 1) * gw] = y

    def finish(c):
        st, rows = c["st"], c["rows"]
        y = jnp.concatenate(c["ys"], axis=1)
        z = z_ref[st, rows, :].astype(F32)
        o_ref[st, rows, :] = (_rms(y * (z * _sigmoid(z))) * g_ref[...]).astype(o_ref.dtype)

    def sub_body(it, carry):
        sub = (subs - 1 - it) if reverse else it
        for st in range(N_STREAMS):
            c = prep(st, sub)
            for g in range(SSM_GROUPS):
                group(c, g)
            if reverse:
                finish(c)
        return carry

    for it in range(subs):
        sub_body(it, 0)


def _streams(a):
    return a.reshape(N_STREAMS, a.shape[0] // N_STREAMS, a.shape[1])


def _stream_spec(nt, reverse):
    def tile_spec(width, col=0):
        if reverse:
            return pl.BlockSpec((N_STREAMS, TOKEN_TILE, width), lambda i: (0, nt - 1 - i, col))
        return pl.BlockSpec((N_STREAMS, TOKEN_TILE, width), lambda i: (0, i, col))
    return tile_spec


def _ssd(xc, bcc, small, dtb_row, a_row, geom, *, reverse, y_fwd=None, proj=None, d_row=None, g_row=None):
    half = xc.shape[1]
    nt = half // TOKEN_TILE
    w = BRANCH_WIDTH
    tile_spec = _stream_spec(nt, reverse)
    in_specs = [tile_spec(w), tile_spec(BC_W), tile_spec(LANES), _resident((1, LANES)), _resident((1, LANES))]
    args = [xc, bcc, small, dtb_row, a_row]
    if reverse:
        in_specs += [tile_spec(w), tile_spec(w, PROJ_BLOCKS["ssm_z"]), _resident((1, w)), _resident((1, w))]
        args += [y_fwd, proj, d_row, g_row]
    return pl.pallas_call(
        functools.partial(_ssd_body, geom=geom, reverse=reverse, n_tiles=nt, half_tokens=half),
        out_shape=jax.ShapeDtypeStruct((N_STREAMS, half, w), BF16 if reverse else F32), grid=(nt,),
        in_specs=in_specs, out_specs=tile_spec(w),
        scratch_shapes=[pltpu.VMEM((N_STREAMS, SSM_GROUPS, SSM_STATE, w // SSM_GROUPS), F32)],
        compiler_params=_cparams(("arbitrary",)), name="ssd_bwd" if reverse else "ssd_fwd")(*args)


def _mlstm_body(*refs, geom, reverse, n_tiles, half_tokens):
    if reverse:
        q_ref, k_ref, v_ref, sm_ref, gb_ref, hf_ref, og_ref, g_ref, o_ref, cn_ref, m_ref = refs
    else:
        q_ref, k_ref, v_ref, sm_ref, gb_ref, o_ref, cn_ref, m_ref = refs
    step = pl.program_id(0)
    tile = (n_tiles - 1 - step) if reverse else step
    mask, tri = _chunk_masks(reverse)
    ig_off = SMALL_IG + (ML_HEADS if reverse else 0)
    fg_off = SMALL_FG + (ML_HEADS if reverse else 0)
    subs = TOKEN_TILE // CHUNK
    k_scale = ML_HEAD_DIM ** -0.5

    def prep(st, sub):
        local, seg = _seg_pos(st * half_tokens + tile * TOKEN_TILE + sub * CHUNK, geom)
        rows = pl.ds(pl.multiple_of(sub * CHUNK, CHUNK), CHUNK)
        gates = sm_ref[st, rows, :] + gb_ref[...]
        fcs = _cumsum_rows(tri, -_softplus(-gates))
        return dict(st=st, rows=rows, start=(local + CHUNK == seg) if reverse else (local == 0),
                    gates=gates, fcs=fcs, fcs_row=fcs.T, gates_row=gates.T,
                    ftot=fcs[0:1, :] if reverse else fcs[CHUNK - 1:CHUNK, :])

    def head(c, h):
        st, rows, start = c["st"], c["rows"], c["start"]
        ci, cf = ig_off + h, fg_off + h
        hsl = slice(h * ML_HEAD_DIM, (h + 1) * ML_HEAD_DIM)
        d = ML_HEAD_DIM
        fc = c["fcs"][:, cf:cf + 1]
        fcs_r = c["fcs_row"][cf:cf + 1, :]
        ig_r = c["gates_row"][ci:ci + 1, :]
        m_st = jnp.where(start, 0.0, m_ref[st, h:h + 1, :])
        r = jnp.where(mask, ig_r - fcs_r, -jnp.inf)
        g = jnp.maximum(jnp.max(r, axis=-1, keepdims=True), m_st)
        sc = jnp.exp(m_st - g)
        qh = q_ref[st, rows, hsl]
        kh = k_ref[st, rows, hsl]
        v1 = jnp.concatenate([v_ref[st, rows, hsl], jnp.ones((CHUNK, d), BF16)], axis=1)
        s = (_dot_nt(qh, kh) * k_scale) * jnp.exp(r - g)
        cn = jnp.where(start, 0.0, cn_ref[st, h])
        both = _dot(s.astype(BF16), v1) + jnp.concatenate([sc, sc], axis=1) * _dot(qh, cn.astype(BF16))
        hh = both[:, :d] / jnp.maximum(jnp.abs(both[:, d:]), jnp.exp(-(fc + g)))
        f_last = jnp.broadcast_to(c["ftot"][:, cf:cf + 1], (1, LANES))
        logw = f_last - fcs_r + ig_r
        m_new = jnp.maximum(f_last + m_st, jnp.max(logw, axis=-1, keepdims=True))
        kw_t = kh.astype(F32).T * (jnp.exp(logw - m_new) * k_scale)
        dec = jnp.exp(f_last + m_st - m_new)
        cn_ref[st, h] = jnp.concatenate([dec, dec], axis=1) * cn + _dot(kw_t.astype(BF16), v1)
        m_ref[st, h:h + 1, :] = m_new
        if reverse:
            hh = (_rms(hf_ref[st, rows, hsl] + hh) * g_ref[:, hsl]
                  * _sigmoid(og_ref[st, rows, hsl].astype(F32)))
        o_ref[st, rows, hsl] = hh.astype(o_ref.dtype)

    def sub_body(it, carry):
        sub = (subs - 1 - it) if reverse else it
        for st in range(N_STREAMS):
            c = prep(st, sub)
            for h in range(ML_HEADS):
                head(c, h)
        return carry

    for it in range(subs):
        sub_body(it, 0)


def _mlstm(proj, small, gate_bias_row, geom, *, reverse, h_fwd=None, g_row=None):
    half = proj.shape[1]
    nt = half // TOKEN_TILE
    w = BRANCH_WIDTH
    tile_spec = _stream_spec(nt, reverse)
    in_specs = [tile_spec(w, PROJ_BLOCKS["ml_q"]), tile_spec(w, PROJ_BLOCKS["ml_k"]),
                tile_spec(w, PROJ_BLOCKS["ml_v"]), tile_spec(LANES), _resident((1, LANES))]
    args = [proj, proj, proj, small, gate_bias_row]
    if reverse:
        in_specs += [tile_spec(w), tile_spec(w, PROJ_BLOCKS["ml_o"]), _resident((1, w))]
        args += [h_fwd, proj, g_row]
    return pl.pallas_call(
        functools.partial(_mlstm_body, geom=geom, reverse=reverse, n_tiles=nt, half_tokens=half),
        out_shape=jax.ShapeDtypeStruct((N_STREAMS, half, w), BF16 if reverse else F32), grid=(nt,),
        in_specs=in_specs, out_specs=tile_spec(w),
        scratch_shapes=[pltpu.VMEM((N_STREAMS, ML_HEADS, ML_HEAD_DIM, 2 * ML_HEAD_DIM), F32),
                        pltpu.VMEM((N_STREAMS, SUBLANES, LANES), F32)],
        compiler_params=_cparams(("arbitrary",)), name="mlstm_bwd" if reverse else "mlstm_fwd")(*args)


def _sgu_body(u_ref, v_ref, g_ref, w_ref, b_ref, o_ref):
    u = _gelu_tanh(u_ref[...].astype(F32))
    vg = (_rms(_gelu_tanh(v_ref[...].astype(F32))) * g_ref[...]).astype(BF16)
    for c in range(TOKEN_TILE // CHUNK):
        rows = slice(c * CHUNK, (c + 1) * CHUNK)
        outs = []
        for g in range(SGU_GROUPS):
            cols = slice(g * SGU_GROUP_DIM, (g + 1) * SGU_GROUP_DIM)
            outs.append(_dot(w_ref[g], vg[rows, cols]) + b_ref[:, g:g + 1])
        o_ref[rows, :] = (u[rows, :] * jnp.concatenate(outs, axis=1)).astype(o_ref.dtype)


def _sgu(proj, g_row, w, b_cols):
    n = proj.shape[0]
    wd = BRANCH_WIDTH
    return pl.pallas_call(
        _sgu_body, out_shape=jax.ShapeDtypeStruct((n, wd), BF16), grid=(n // TOKEN_TILE,),
        in_specs=[pl.BlockSpec((TOKEN_TILE, wd), lambda i: (i, PROJ_BLOCKS["sgu_u"])),
                  pl.BlockSpec((TOKEN_TILE, wd), lambda i: (i, PROJ_BLOCKS["sgu_v"])),
                  _resident((1, wd)), _resident(w.shape), _resident(b_cols.shape)],
        out_specs=pl.BlockSpec((TOKEN_TILE, wd), lambda i: (i, 0)),
        compiler_params=_cparams(("parallel",)), name="sgu")(proj, proj, g_row, w, b_cols)


def _merge_body(x_ref, gm_ref, wg_ref, bg_ref, y0_ref, y1_ref, y2_ref, y3_ref, wb_ref, wo_ref, o_ref):
    x = x_ref[...]
    h = (_rms(x) * gm_ref[...]).astype(BF16)
    merged = None
    for i, y_ref in enumerate((y0_ref, y1_ref, y2_ref, y3_ref)):
        cols = slice(i * D_MODEL, (i + 1) * D_MODEL)
        term = _sigmoid(_dot(h, wg_ref[:, cols]) + bg_ref[:, cols]) * _dot(y_ref[...], wb_ref[i])
        merged = term if merged is None else merged + term
    o_ref[...] = x + _dot(merged.astype(BF16), wo_ref[...])


def _merge(x, gm_row, wg, bg_row, branches, wb, wo):
    n, d = x.shape
    br_spec = pl.BlockSpec((TOKEN_TILE, BRANCH_WIDTH), lambda i: (i, 0))
    x_spec = pl.BlockSpec((TOKEN_TILE, d), lambda i: (i, 0))
    return pl.pallas_call(
        _merge_body, out_shape=jax.ShapeDtypeStruct((n, d), F32), grid=(n // TOKEN_TILE,),
        in_specs=[x_spec, _resident((1, d)), _resident(wg.shape), _resident(bg_row.shape),
                  br_spec, br_spec, br_spec, br_spec, _resident(wb.shape), _resident(wo.shape)],
        out_specs=x_spec, compiler_params=_cparams(("parallel",)), name="merge")(
            x, gm_row, wg, bg_row, *branches, wb, wo)


def _ffn_body(x_ref, xp_ref, xn_ref, g_ref, wu_ref, cw_ref, cb_ref, wd_ref, gf_ref, o_ref, act_ref, sa_ref,
              sb_ref, *, geom, final):
    first, last = _tile_ends(geom)
    t = TOKEN_TILE
    hp = SUBLANES
    x = x_ref[...]
    h = (_rms(jnp.concatenate([xp_ref[...], x, xn_ref[...]], axis=0)) * g_ref[...]).astype(BF16)
    pad = FFN_CONV_W // 2

    def project(s_ref, slot, cols):
        up = _dot(h, wu_ref[:, cols])
        s_ref[slot, 0:hp, :] = jnp.where(first, 0.0, up[0:hp, :])
        s_ref[slot, hp:hp + t, :] = up[hp:hp + t, :]
        s_ref[slot, hp + t:, :] = jnp.where(last, 0.0, up[hp + t:, :])

    def project_chunk(c):
        c0 = c * FFN_COL_CHUNK
        project(sa_ref, c % 2, slice(c0, c0 + FFN_COL_CHUNK))
        project(sb_ref, c % 2, slice(D_FF + c0, D_FF + c0 + FFN_COL_CHUNK))

    def conv3(s_ref, slot, cols):
        acc = cb_ref[:, cols]
        for k in range(FFN_CONV_W):
            acc = acc + cw_ref[k:k + 1, cols] * s_ref[slot, hp + k - pad:hp + k - pad + t, :]
        return acc

    o_ref[...] = x
    n_chunks = D_FF // FFN_COL_CHUNK
    per_group = FFN_DOWN_GROUP // FFN_COL_CHUNK
    project_chunk(0)
    for c in range(n_chunks):
        if c + 1 < n_chunks:
            project_chunk(c + 1)
        c0 = c * FFN_COL_CHUNK
        a = conv3(sa_ref, c % 2, slice(c0, c0 + FFN_COL_CHUNK))
        b = conv3(sb_ref, c % 2, slice(D_FF + c0, D_FF + c0 + FFN_COL_CHUNK))
        k0 = (c % per_group) * FFN_COL_CHUNK
        grp = c // per_group
        act_ref[grp % 2, :, k0:k0 + FFN_COL_CHUNK] = (_gelu_tanh(a) * b).astype(BF16)
        if c % per_group == per_group - 1 or c == n_chunks - 1:
            g0 = grp * FFN_DOWN_GROUP
            o_ref[...] += _dot(act_ref[grp % 2, :, :k0 + FFN_COL_CHUNK], wd_ref[g0:c0 + FFN_COL_CHUNK, :])
    if final:
        o_ref[...] = _rms(o_ref[...]) * gf_ref[...]


def _ffn(x, g_row, w_up, conv_w, conv_b, w_down, gf_row, geom, *, final):
    assert conv_w.shape[0] == FFN_CONV_W
    n, d = x.shape
    per = TOKEN_TILE // SUBLANES
    nh = n // SUBLANES
    x_spec = pl.BlockSpec((TOKEN_TILE, d), lambda i: (i, 0))
    halo_rows = TOKEN_TILE + 2 * SUBLANES
    return pl.pallas_call(
        functools.partial(_ffn_body, geom=geom, final=final),
        out_shape=jax.ShapeDtypeStruct((n, d), F32), grid=(n // TOKEN_TILE,),
        in_specs=[x_spec,
                  pl.BlockSpec((SUBLANES, d), lambda i: (jnp.maximum(i * per - 1, 0), 0)),
                  pl.BlockSpec((SUBLANES, d), lambda i: (jnp.minimum((i + 1) * per, nh - 1), 0)),
                  _resident((1, d)), _resident(w_up.shape), _resident(conv_w.shape),
                  _resident((1, 2 * D_FF)), _resident(w_down.shape), _resident((1, d))],
        out_specs=x_spec,
        scratch_shapes=[pltpu.VMEM((2, TOKEN_TILE, FFN_DOWN_GROUP), BF16),
                        pltpu.VMEM((2, halo_rows, FFN_COL_CHUNK), F32),
                        pltpu.VMEM((2, halo_rows, FFN_COL_CHUNK), F32)],
        compiler_params=_cparams(("parallel",)), name="ffn")(
            x, x, x, g_row, w_up, conv_w, conv_b.reshape(1, 2 * D_FF), w_down, gf_row)


def _pad_row(vec, offset):
    return jnp.zeros((1, LANES), F32).at[0, offset:offset + vec.shape[0]].set(vec.astype(F32))


def kernel(x_prompt, x_sample, norm_mix_g, w_in, na_rpb, ssm_conv_w, ssm_conv_b, ssm_dt_bias, ssm_a_log, ssm_d, ssm_norm_g, ml_igate_b, ml_fgate_b, ml_norm_g, sgu_norm_g, sgu_w, sgu_b, w_branch, w_gate, b_gate, w_out, norm_ffn_g, ffn_w_up, ffn_conv_w, ffn_conv_b, ffn_w_down, norm_final_g):
    bp, lp, d = x_prompt.shape
    bs, ls, _ = x_sample.shape
    assert d == D_MODEL and lp % TOKEN_TILE == 0 and ls % TOKEN_TILE == 0
    assert min(lp, ls) >= 2 * TOKEN_TILE
    geom = (bp * lp, lp, ls)
    n = bp * lp + bs * ls
    for st in range(1, N_STREAMS):
        split = st * n // N_STREAMS
        assert n % N_STREAMS == 0 and (split % lp == 0 if split <= bp * lp else (split - bp * lp) % ls == 0)
    depth = w_in.shape[0]
    x = None

    o_ssm, o_ml, o_sgu = NA_COLS, NA_COLS + SSM_COLS, NA_COLS + SSM_COLS + ML_COLS
    bc_lo = o_ssm + 2 * BRANCH_WIDTH
    dt_lo = bc_lo + BC_W
    mlg_lo = o_ml + 4 * BRANCH_WIDTH
    for l in range(depth):
        wl = w_in[l]
        w_main = jnp.concatenate([wl[:, :bc_lo], wl[:, o_ml:mlg_lo], wl[:, o_sgu:], wl[:, bc_lo:dt_lo]],
                                 axis=1).astype(BF16)
        w_small = jnp.concatenate([wl[:, dt_lo:o_ml], wl[:, mlg_lo:o_sgu],
                                   jnp.zeros((d, LANES - 2 * SSM_HEADS - 4 * ML_HEADS), F32)],
                                  axis=1).astype(BF16)
        if l == 0:
            proj, small, x = _in_proj((x_prompt.reshape(bp * lp, d), x_sample.reshape(bs * ls, d)),
                                      norm_mix_g[l], w_main, w_small)
        else:
            proj, small = _in_proj((x,), norm_mix_g[l], w_main, w_small)

        y_na = _neighbourhood_attention(proj, _na_bias_table(na_rpb[l]), geom)

        xc = _conv_silu(proj, PROJ_BLOCKS["ssm_x"], BRANCH_WIDTH, ssm_conv_w[l][:, :BRANCH_WIDTH],
                        ssm_conv_b[l][:BRANCH_WIDTH], geom, name="ssm_conv_x")
        bcc = _conv_silu(proj, BC_BLOCK, BC_W, ssm_conv_w[l][:, BRANCH_WIDTH:], ssm_conv_b[l][BRANCH_WIDTH:],
                         geom, name="ssm_conv_bc")
        dtb_row = _pad_row(ssm_dt_bias[l].reshape(-1), SMALL_DT)
        a_neg = -jnp.exp(ssm_a_log[l].astype(F32))
        proj_s, small_s, xc_s, bcc_s = _streams(proj), _streams(small), _streams(xc), _streams(bcc)
        y_f = _ssd(xc_s, bcc_s, small_s, dtb_row, _pad_row(a_neg[0], SMALL_DT), geom, reverse=False)
        d_row = jnp.repeat(ssm_d[l].astype(F32), SSM_HEAD_DIM).reshape(1, BRANCH_WIDTH)
        y_ssm = _ssd(xc_s, bcc_s, small_s, dtb_row, _pad_row(a_neg[1], SMALL_DT + SSM_HEADS), geom, reverse=True,
                     y_fwd=y_f, proj=proj_s, d_row=d_row, g_row=ssm_norm_g[l].reshape(1, BRANCH_WIDTH))
        y_ssm = y_ssm.reshape(n, BRANCH_WIDTH)

        gate_bias = _pad_row(jnp.concatenate([ml_igate_b[l].reshape(-1), ml_fgate_b[l].reshape(-1)]), SMALL_IG)
        h_f = _mlstm(proj_s, small_s, gate_bias, geom, reverse=False)
        y_ml = _mlstm(proj_s, small_s, gate_bias, geom, reverse=True, h_fwd=h_f,
                      g_row=ml_norm_g[l].reshape(1, BRANCH_WIDTH)).reshape(n, BRANCH_WIDTH)

        sgu_b_cols = jnp.zeros((CHUNK, LANES), F32).at[:, :SGU_GROUPS].set(sgu_b[l].T.astype(F32))
        y_sgu = _sgu(proj, sgu_norm_g[l].reshape(1, BRANCH_WIDTH), sgu_w[l].astype(BF16), sgu_b_cols)

        x = _merge(x, norm_mix_g[l].reshape(1, d), w_gate[l].astype(BF16), b_gate[l].reshape(1, N_BRANCH * d),
                   (y_na, y_ssm, y_ml, y_sgu), w_branch[l].astype(BF16), w_out[l].astype(BF16))

        x = _ffn(x, norm_ffn_g[l].reshape(1, d), ffn_w_up[l].astype(BF16), ffn_conv_w[l], ffn_conv_b[l],
                 ffn_w_down[l].astype(BF16), norm_final_g.reshape(1, d), geom, final=(l == depth - 1))

    return (x[:bp * lp].reshape(bp, lp, d), x[bp * lp:].reshape(bs, ls, d))
```

```python
import functools
import math

import jax
import jax.numpy as jnp
import numpy as np
from jax import lax
from jax.experimental import pallas as pl
from jax.experimental.pallas import tpu as pltpu

F32 = jnp.float32
BF16 = jnp.bfloat16

D_MODEL = 1024
GRID_W = 64
BRANCH_WIDTH = 512
N_BRANCH = 4
NA_HEADS = 8
NA_HEAD_DIM = BRANCH_WIDTH // NA_HEADS
NA_WR = 8
NA_WC = 16
SSM_HEADS = 8
SSM_HEAD_DIM = BRANCH_WIDTH // SSM_HEADS
SSM_GROUPS = 2
SSM_STATE = 64
SSM_CONV_W = 5
CHUNK = 128
XBC_W = BRANCH_WIDTH + 2 * SSM_GROUPS * SSM_STATE
ML_HEADS = 4
ML_HEAD_DIM = BRANCH_WIDTH // ML_HEADS
SGU_GROUPS = 4
SGU_GROUP_DIM = BRANCH_WIDTH // SGU_GROUPS
D_FF = 2816
FFN_CONV_W = 3
NORM_EPS = 1e-6
NA_COLS = 3 * BRANCH_WIDTH
SSM_COLS = BRANCH_WIDTH + XBC_W + 2 * SSM_HEADS
ML_COLS = 4 * BRANCH_WIDTH + 4 * ML_HEADS

LANES = 128
SUBLANES = 8
BF16_ROWS = 16
VMEM_LIMIT = 56 * 1024 * 1024

TOKEN_TILE = 512
N_STREAMS = 2
NA_BLOCK_ROWS = TOKEN_TILE // GRID_W
FFN_COL_CHUNK = 256
FFN_DOWN_GROUP = 1024
MASK_VALUE = -1e30

PROJ_BLOCKS = dict(na_q=0, na_k=1, na_v=2, ssm_z=3, ssm_x=4, ml_q=5, ml_k=6, ml_v=7, ml_o=8,
                   sgu_u=9, sgu_v=10)
PROJ_MAIN_W = 11 * BRANCH_WIDTH + 2 * SSM_GROUPS * SSM_STATE
BC_W = 2 * SSM_GROUPS * SSM_STATE
BC_BLOCK = (11 * BRANCH_WIDTH) // BC_W
SMALL_DT = 0
SMALL_IG = 2 * SSM_HEADS
SMALL_FG = SMALL_IG + 2 * ML_HEADS


def _cparams(sem):
    return pltpu.CompilerParams(dimension_semantics=sem, vmem_limit_bytes=VMEM_LIMIT)


def _resident(shape):
    nd = len(shape)
    return pl.BlockSpec(shape, lambda *_: (0,) * nd, pipeline_mode=pl.Buffered(1))


def _seg_pos(pos, geom):
    n_prompt, len_prompt, len_sample = geom
    in_prompt = pos < n_prompt
    local = jnp.where(in_prompt, pos % len_prompt, (pos - n_prompt) % len_sample)
    return local, jnp.where(in_prompt, len_prompt, len_sample)


def _rms(x):
    return x * lax.rsqrt(jnp.mean(x * x, axis=-1, keepdims=True) + NORM_EPS)


def _sigmoid(x):
    return 1.0 / (1.0 + jnp.exp(-x))


def _softplus(x):
    return jnp.maximum(x, 0.0) + jnp.log(1.0 + jnp.exp(-jnp.abs(x)))


def _gelu_tanh(x):
    return 0.5 * x * (1.0 + jnp.tanh(math.sqrt(2.0 / math.pi) * (x + 0.044715 * (x * x * x))))


def _dot(a, b):
    return jnp.dot(a, b, preferred_element_type=F32)


def _dot_nt(a, b):
    return lax.dot_general(a, b, (((1,), (1,)), ((), ())), preferred_element_type=F32)


def _cumsum_rows(tri, x):
    return jnp.dot(tri, x, preferred_element_type=F32, precision=lax.Precision.HIGHEST)


def _chunk_masks(reverse):
    t = lax.broadcasted_iota(jnp.int32, (CHUNK, CHUNK), 0)
    s = lax.broadcasted_iota(jnp.int32, (CHUNK, CHUNK), 1)
    mask = (s >= t) if reverse else (s <= t)
    return mask, jnp.where(mask, 1.0, 0.0).astype(F32)


def _norm_matmul_body(*refs, chunks, split):
    if split is None:
        x_ref, g_ref, w_ref, ws_ref, o_ref, os_ref = refs
        x = x_ref[...]
    else:
        xa_ref, xb_ref, g_ref, w_ref, ws_ref, o_ref, os_ref, xo_ref = refs
        x = jnp.where(pl.program_id(0) < split, xa_ref[...], xb_ref[...])
        xo_ref[...] = x
    h = (_rms(x) * g_ref[...]).astype(BF16)
    for c0, cw in chunks:
        o_ref[:, c0:c0 + cw] = _dot(h, w_ref[:, c0:c0 + cw]).astype(o_ref.dtype)
    os_ref[...] = _dot(h, ws_ref[...])


def _col_chunks(width, step=512):
    out, c = [], 0
    while c < width:
        out.append((c, min(step, width - c)))
        c += step
    return tuple(out)


def _in_proj(xs, g, w, w_small):
    d = w.shape[0]
    cols, cols_small = w.shape[1], w_small.shape[1]
    n = sum(a.shape[0] for a in xs)
    row = lambda width: pl.BlockSpec((TOKEN_TILE, width), lambda i: (i, 0))
    if len(xs) == 1:
        split = None
        x_specs = [row(d)]
    else:
        split = xs[0].shape[0] // TOKEN_TILE
        last_b = xs[1].shape[0] // TOKEN_TILE - 1
        x_specs = [pl.BlockSpec((TOKEN_TILE, d), lambda i: (jnp.minimum(i, split - 1), 0)),
                   pl.BlockSpec((TOKEN_TILE, d), lambda i: (jnp.clip(i - split, 0, last_b), 0))]
    out_shape = [jax.ShapeDtypeStruct((n, cols), BF16), jax.ShapeDtypeStruct((n, cols_small), F32)]
    out_specs = [row(cols), row(cols_small)]
    if split is not None:
        out_shape.append(jax.ShapeDtypeStruct((n, d), F32))
        out_specs.append(row(d))
    return pl.pallas_call(
        functools.partial(_norm_matmul_body, chunks=_col_chunks(cols), split=split),
        out_shape=out_shape, grid=(n // TOKEN_TILE,),
        in_specs=x_specs + [_resident((1, d)), _resident(w.shape), _resident(w_small.shape)],
        out_specs=out_specs, compiler_params=_cparams(("parallel",)), name="in_proj")(
            *xs, g.reshape(1, d), w, w_small)


def _na_bias_table(rpb):
    cols = np.arange(GRID_W)
    col_start = np.clip(cols - NA_WC // 2, 0, GRID_W - NA_WC)
    valid = (cols[None, :] >= col_start[:, None]) & (cols[None, :] < col_start[:, None] + NA_WC)
    sel_r = np.zeros((2 * NA_WR - 1, NA_WR, NA_WR), np.float32)
    off, i = np.meshgrid(np.arange(NA_WR), np.arange(NA_WR), indexing="ij")
    sel_r[i - off + (NA_WR - 1), off, i] = 1.0
    sel_c = np.zeros((2 * NA_WC - 1, GRID_W, GRID_W), np.float32)
    q, kc = np.nonzero(valid)
    sel_c[kc - q + (NA_WC - 1), q, kc] = 1.0
    t = jnp.einsum("hrc,roi,cqk->ohqik", rpb.astype(F32), sel_r, sel_c, precision=lax.Precision.HIGHEST)
    t = jnp.where(jnp.asarray(valid)[None, None, :, None, :], t, MASK_VALUE)
    return t.reshape(NA_WR, NA_HEADS * GRID_W, NA_WR * GRID_W)


def _na_window_start(i, n_tokens):
    return jnp.clip(i - 1, 0, n_tokens // TOKEN_TILE - 3) * TOKEN_TILE


def _na_body(q_ref, kbuf, vbuf, bias_ref, o_ref, s_ref, p_ref, *, geom, n_tokens):
    i = pl.program_id(0)
    local, seg = _seg_pos(i * TOKEN_TILE, geom)
    first = local == 0
    last = local + TOKEN_TILE == seg
    base_row = (i * TOKEN_TILE - _na_window_start(i, n_tokens)) // GRID_W
    lo = jnp.where(first, 0, -NA_WR)
    hi = jnp.where(last, 0, NA_WR)
    win = NA_WR * GRID_W
    low = lax.broadcasted_iota(jnp.int32, (GRID_W, LANES), 1) < NA_HEAD_DIM
    pair_rows = 2 * GRID_W

    def row_body(j, carry):
        start = base_row + jnp.clip(j - NA_WR // 2, lo, hi)
        off = base_row + j - start
        r0 = pl.multiple_of(start * GRID_W, GRID_W)
        q0 = j * GRID_W
        qrow = q_ref[pl.ds(q0, GRID_W), :].astype(F32) * NA_HEAD_DIM ** -0.5
        for hp in range(NA_HEADS // 2):
            cols = slice(hp * LANES, (hp + 1) * LANES)
            rows = slice(hp * pair_rows, (hp + 1) * pair_rows)
            qp = qrow[:, cols]
            q2 = jnp.concatenate([jnp.where(low, qp, 0.0), jnp.where(low, 0.0, qp)], axis=0).astype(BF16)
            s_ref[j % 2, rows, :] = _dot_nt(q2, kbuf[pl.ds(r0, win), cols]) + bias_ref[off, rows, :]
        s = s_ref[j % 2]
        p = jnp.exp(s - jnp.max(s, axis=-1, keepdims=True))
        inv = 1.0 / jnp.sum(p, axis=-1, keepdims=True)
        p_ref[j % 2] = p.astype(BF16)
        outs = []
        for hp in range(NA_HEADS // 2):
            cols = slice(hp * LANES, (hp + 1) * LANES)
            rows = slice(hp * pair_rows, (hp + 1) * pair_rows)
            o2 = _dot(p_ref[j % 2, rows, :], vbuf[pl.ds(r0, win), cols]) * inv[rows, :]
            outs.append(jnp.where(low, o2[:GRID_W, :], o2[GRID_W:, :]))
        o_ref[pl.ds(q0, GRID_W), :] = jnp.concatenate(outs, axis=1).astype(o_ref.dtype)
        return carry

    for j in range(NA_BLOCK_ROWS):
        row_body(j, 0)


def _neighbourhood_attention(proj, bias_table, geom):
    n = proj.shape[0]
    nb = n // TOKEN_TILE
    w = BRANCH_WIDTH

    def window_spec(col):
        return pl.BlockSpec((pl.Element(3 * TOKEN_TILE), pl.Element(w)),
                            lambda i: (_na_window_start(i, n), col * w))

    kq, kk, kv = PROJ_BLOCKS["na_q"], PROJ_BLOCKS["na_k"], PROJ_BLOCKS["na_v"]
    return pl.pallas_call(
        functools.partial(_na_body, geom=geom, n_tokens=n),
        out_shape=jax.ShapeDtypeStruct((n, w), BF16), grid=(nb,),
        in_specs=[pl.BlockSpec((TOKEN_TILE, w), lambda i: (i, kq)), window_spec(kk), window_spec(kv),
                  _resident(bias_table.shape)],
        out_specs=pl.BlockSpec((TOKEN_TILE, w), lambda i: (i, 0)),
        scratch_shapes=[pltpu.VMEM((2, NA_HEADS * GRID_W, NA_WR * GRID_W), F32),
                        pltpu.VMEM((2, NA_HEADS * GRID_W, NA_WR * GRID_W), BF16)],
        compiler_params=_cparams(("parallel",)), name="na_attention")(proj, proj, proj, bias_table)


def _dwconv_tile(scr_ref, cur, prev, nxt, w, bias, first, last):
    t = cur.shape[0]
    kw = w.shape[0]
    pad = kw // 2
    scr_ref[0:SUBLANES, :] = jnp.where(first, 0.0, prev[BF16_ROWS - SUBLANES:, :])
    scr_ref[SUBLANES:SUBLANES + t, :] = cur
    scr_ref[SUBLANES + t:2 * SUBLANES + t, :] = jnp.where(last, 0.0, nxt[:SUBLANES, :])
    acc = bias
    for k in range(kw):
        r = SUBLANES + k - pad
        acc = acc + w[k:k + 1, :] * scr_ref[r:r + t, :]
    return acc


def _halo_specs(n, width, col):
    per = TOKEN_TILE // BF16_ROWS
    nh = n // BF16_ROWS
    return [pl.BlockSpec((TOKEN_TILE, width), lambda i: (i, col)),
            pl.BlockSpec((BF16_ROWS, width), lambda i: (jnp.maximum(i * per - 1, 0), col)),
            pl.BlockSpec((BF16_ROWS, width), lambda i: (jnp.minimum((i + 1) * per, nh - 1), col))]


def _tile_ends(geom):
    local, seg = _seg_pos(pl.program_id(0) * TOKEN_TILE, geom)
    return local == 0, local + TOKEN_TILE == seg


def _conv_silu_body(c_ref, p_ref, n_ref, w_ref, b_ref, o_ref, scr_ref, *, geom):
    first, last = _tile_ends(geom)
    y = _dwconv_tile(scr_ref, c_ref[...].astype(F32), p_ref[...].astype(F32), n_ref[...].astype(F32),
                     w_ref[...], b_ref[...], first, last)
    o_ref[...] = (y * _sigmoid(y)).astype(o_ref.dtype)


def _conv_silu(proj, col, width, w, b, geom, *, name):
    n = proj.shape[0]
    return pl.pallas_call(
        functools.partial(_conv_silu_body, geom=geom),
        out_shape=jax.ShapeDtypeStruct((n, width), BF16), grid=(n // TOKEN_TILE,),
        in_specs=_halo_specs(n, width, col) + [_resident(w.shape), _resident((1, width))],
        out_specs=pl.BlockSpec((TOKEN_TILE, width), lambda i: (i, 0)),
        scratch_shapes=[pltpu.VMEM((TOKEN_TILE + 2 * SUBLANES, width), F32)],
        compiler_params=_cparams(("parallel",)), name=name)(proj, proj, proj, w, b.reshape(1, width))


def _ssd_body(*refs, geom, reverse, n_tiles, half_tokens):
    if reverse:
        x_ref, bc_ref, sm_ref, dtb_ref, a_ref, yf_ref, z_ref, d_ref, g_ref, o_ref, h_ref = refs
    else:
        x_ref, bc_ref, sm_ref, dtb_ref, a_ref, o_ref, h_ref = refs
    step = pl.program_id(0)
    tile = (n_tiles - 1 - step) if reverse else step
    mask, tri = _chunk_masks(reverse)
    dir_off = SMALL_DT + (SSM_HEADS if reverse else 0)
    hg = SSM_HEADS // SSM_GROUPS
    gw = hg * SSM_HEAD_DIM
    subs = TOKEN_TILE // CHUNK

    def prep(st, sub):
        local, seg = _seg_pos(st * half_tokens + tile * TOKEN_TILE + sub * CHUNK, geom)
        rows = pl.ds(pl.multiple_of(sub * CHUNK, CHUNK), CHUNK)
        bc = bc_ref[st, rows, :].astype(F32)
        dt = _softplus(sm_ref[st, rows, :] + dtb_ref[...])
        cs = _cumsum_rows(tri, dt * a_ref[...])
        return dict(st=st, rows=rows, start=(local + CHUNK == seg) if reverse else (local == 0),
                    bc=bc, bc_t=bc.T, dt=dt, dt_row=dt.T, cs=cs, cs_row=cs.T,
                    tot=cs[0:1, :] if reverse else cs[CHUNK - 1:CHUNK, :], ys=[])

    def group(c, g):
        st, rows, bc, cs, dt = c["st"], c["rows"], c["bc"], c["cs"], c["dt"]
        b_g = bc[:, g * SSM_STATE:(g + 1) * SSM_STATE].astype(BF16)
        c_lo = SSM_GROUPS * SSM_STATE + g * SSM_STATE
        c_g = bc[:, c_lo:c_lo + SSM_STATE].astype(BF16)
        b_gt = c["bc_t"][g * SSM_STATE:(g + 1) * SSM_STATE, :]
        cb = _dot_nt(c_g, b_g)
        x = x_ref[st, rows, g * gw:(g + 1) * gw]
        h_prev = jnp.where(c["start"], 0.0, h_ref[st, g])
        y_off = _dot(c_g, h_prev.astype(BF16))
        low = lax.broadcasted_iota(jnp.int32, (CHUNK, LANES), 1) < SSM_HEAD_DIM
        head_of_lane = lax.broadcasted_iota(jnp.int32, (1, gw), 1) // SSM_HEAD_DIM
        y_pairs, lhs, dec = [], [], jnp.zeros((1, gw), F32)
        for p in range(hg // 2):
            ps = slice(p * LANES, (p + 1) * LANES)
            yd, ea = [], []
            for j in (2 * p, 2 * p + 1):
                ch = dir_off + g * hg + j
                cs_r = c["cs_row"][ch:ch + 1, :]
                dt_r = c["dt_row"][ch:ch + 1, :]
                a_rep = jnp.broadcast_to(cs[:, ch:ch + 1], (CHUNK, CHUNK))
                decay = jnp.exp(jnp.where(mask, a_rep - cs_r, -jnp.inf))
                yd.append(_dot((cb * decay * dt_r).astype(BF16), x[:, ps]))
                ea.append(jnp.exp(a_rep))
                tot_h = jnp.broadcast_to(c["tot"][:, ch:ch + 1], (1, CHUNK))
                lhs.append(b_gt * (jnp.exp(tot_h - cs_r) * dt_r))
                dec = jnp.where(head_of_lane == j, jnp.exp(c["tot"][:, ch:ch + 1]), dec)
            y_pairs.append(jnp.where(low, yd[0], yd[1]) + y_off[:, ps] * jnp.where(low, ea[0], ea[1]))
        upd = _dot(jnp.concatenate(lhs, axis=0).astype(BF16), x)
        new = jnp.zeros((SSM_STATE, gw), F32)
        for j in range(hg):
            new = jnp.where(head_of_lane == j, upd[j * SSM_STATE:(j + 1) * SSM_STATE, :], new)
        h_ref[st, g] = h_prev * dec + new
        y = jnp.concatenate(y_pairs, axis=1)
        if reverse:
            c["ys"].append(yf_ref[st, rows, g * gw:(g + 1) * gw] + y
                           + x.astype(F32) * d_ref[:, g * gw:(g + 1) * gw])
        else:
            o_ref[st, rows, g * gw:(g + 1) * gw] = y

    def finish(c):
        st, rows = c["st"], c["rows"]
        y = jnp.concatenate(c["ys"], axis=1)
        z = z_ref[st, rows, :].astype(F32)
        o_ref[st, rows, :] = (_rms(y * (z * _sigmoid(z))) * g_ref[...]).astype(o_ref.dtype)

    def sub_body(it, carry):
        sub = (subs - 1 - it) if reverse else it
        for st in range(N_STREAMS):
            c = prep(st, sub)
            for g in range(SSM_GROUPS):
                group(c, g)
            if reverse:
                finish(c)
        return carry

    for it in range(subs):
        sub_body(it, 0)


def _streams(a):
    return a.reshape(N_STREAMS, a.shape[0] // N_STREAMS, a.shape[1])


def _stream_spec(nt, reverse):
    def tile_spec(width, col=0):
        if reverse:
            return pl.BlockSpec((N_STREAMS, TOKEN_TILE, width), lambda i: (0, nt - 1 - i, col))
        return pl.BlockSpec((N_STREAMS, TOKEN_TILE, width), lambda i: (0, i, col))
    return tile_spec


def _ssd(xc, bcc, small, dtb_row, a_row, geom, *, reverse, y_fwd=None, proj=None, d_row=None, g_row=None):
    half = xc.shape[1]
    nt = half // TOKEN_TILE
    w = BRANCH_WIDTH
    tile_spec = _stream_spec(nt, reverse)
    in_specs = [tile_spec(w), tile_spec(BC_W), tile_spec(LANES), _resident((1, LANES)), _resident((1, LANES))]
    args = [xc, bcc, small, dtb_row, a_row]
    if reverse:
        in_specs += [tile_spec(w), tile_spec(w, PROJ_BLOCKS["ssm_z"]), _resident((1, w)), _resident((1, w))]
        args += [y_fwd, proj, d_row, g_row]
    return pl.pallas_call(
        functools.partial(_ssd_body, geom=geom, reverse=reverse, n_tiles=nt, half_tokens=half),
        out_shape=jax.ShapeDtypeStruct((N_STREAMS, half, w), BF16 if reverse else F32), grid=(nt,),
        in_specs=in_specs, out_specs=tile_spec(w),
        scratch_shapes=[pltpu.VMEM((N_STREAMS, SSM_GROUPS, SSM_STATE, w // SSM_GROUPS), F32)],
        compiler_params=_cparams(("arbitrary",)), name="ssd_bwd" if reverse else "ssd_fwd")(*args)


def _mlstm_body(*refs, geom, reverse, n_tiles, half_tokens):
    if reverse:
        q_ref, k_ref, v_ref, sm_ref, gb_ref, hf_ref, og_ref, g_ref, o_ref, cn_ref, m_ref = refs
    else:
        q_ref, k_ref, v_ref, sm_ref, gb_ref, o_ref, cn_ref, m_ref = refs
    step = pl.program_id(0)
    tile = (n_tiles - 1 - step) if reverse else step
    mask, tri = _chunk_masks(reverse)
    ig_off = SMALL_IG + (ML_HEADS if reverse else 0)
    fg_off = SMALL_FG + (ML_HEADS if reverse else 0)
    subs = TOKEN_TILE // CHUNK
    k_scale = ML_HEAD_DIM ** -0.5

    def prep(st, sub):
        local, seg = _seg_pos(st * half_tokens + tile * TOKEN_TILE + sub * CHUNK, geom)
        rows = pl.ds(pl.multiple_of(sub * CHUNK, CHUNK), CHUNK)
        gates = sm_ref[st, rows, :] + gb_ref[...]
        fcs = _cumsum_rows(tri, -_softplus(-gates))
        return dict(st=st, rows=rows, start=(local + CHUNK == seg) if reverse else (local == 0),
                    gates=gates, fcs=fcs, fcs_row=fcs.T, gates_row=gates.T,
                    ftot=fcs[0:1, :] if reverse else fcs[CHUNK - 1:CHUNK, :])

    def head(c, h):
        st, rows, start = c["st"], c["rows"], c["start"]
        ci, cf = ig_off + h, fg_off + h
        hsl = slice(h * ML_HEAD_DIM, (h + 1) * ML_HEAD_DIM)
        d = ML_HEAD_DIM
        fc = c["fcs"][:, cf:cf + 1]
        fcs_r = c["fcs_row"][cf:cf + 1, :]
        ig_r = c["gates_row"][ci:ci + 1, :]
        m_st = jnp.where(start, 0.0, m_ref[st, h:h + 1, :])
        r = jnp.where(mask, ig_r - fcs_r, -jnp.inf)
        g = jnp.maximum(jnp.max(r, axis=-1, keepdims=True), m_st)
        sc = jnp.exp(m_st - g)
        qh = q_ref[st, rows, hsl]
        kh = k_ref[st, rows, hsl]
        v1 = jnp.concatenate([v_ref[st, rows, hsl], jnp.ones((CHUNK, d), BF16)], axis=1)
        s = (_dot_nt(qh, kh) * k_scale) * jnp.exp(r - g)
        cn = jnp.where(start, 0.0, cn_ref[st, h])
        both = _dot(s.astype(BF16), v1) + jnp.concatenate([sc, sc], axis=1) * _dot(qh, cn.astype(BF16))
        hh = both[:, :d] / jnp.maximum(jnp.abs(both[:, d:]), jnp.exp(-(fc + g)))
        f_last = jnp.broadcast_to(c["ftot"][:, cf:cf + 1], (1, LANES))
        logw = f_last - fcs_r + ig_r
        m_new = jnp.maximum(f_last + m_st, jnp.max(logw, axis=-1, keepdims=True))
        kw_t = kh.astype(F32).T * (jnp.exp(logw - m_new) * k_scale)
        dec = jnp.exp(f_last + m_st - m_new)
        cn_ref[st, h] = jnp.concatenate([dec, dec], axis=1) * cn + _dot(kw_t.astype(BF16), v1)
        m_ref[st, h:h + 1, :] = m_new
        if reverse:
            hh = (_rms(hf_ref[st, rows, hsl] + hh) * g_ref[:, hsl]
                  * _sigmoid(og_ref[st, rows, hsl].astype(F32)))
        o_ref[st, rows, hsl] = hh.astype(o_ref.dtype)

    def sub_body(it, carry):
        sub = (subs - 1 - it) if reverse else it
        for st in range(N_STREAMS):
            c = prep(st, sub)
            for h in range(ML_HEADS):
                head(c, h)
        return carry

    for it in range(subs):
        sub_body(it, 0)


def _mlstm(proj, small, gate_bias_row, geom, *, reverse, h_fwd=None, g_row=None):
    half = proj.shape[1]
    nt = half // TOKEN_TILE
    w = BRANCH_WIDTH
    tile_spec = _stream_spec(nt, reverse)
    in_specs = [tile_spec(w, PROJ_BLOCKS["ml_q"]), tile_spec(w, PROJ_BLOCKS["ml_k"]),
                tile_spec(w, PROJ_BLOCKS["ml_v"]), tile_spec(LANES), _resident((1, LANES))]
    args = [proj, proj, proj, small, gate_bias_row]
    if reverse:
        in_specs += [tile_spec(w), tile_spec(w, PROJ_BLOCKS["ml_o"]), _resident((1, w))]
        args += [h_fwd, proj, g_row]
    return pl.pallas_call(
        functools.partial(_mlstm_body, geom=geom, reverse=reverse, n_tiles=nt, half_tokens=half),
        out_shape=jax.ShapeDtypeStruct((N_STREAMS, half, w), BF16 if reverse else F32), grid=(nt,),
        in_specs=in_specs, out_specs=tile_spec(w),
        scratch_shapes=[pltpu.VMEM((N_STREAMS, ML_HEADS, ML_HEAD_DIM, 2 * ML_HEAD_DIM), F32),
                        pltpu.VMEM((N_STREAMS, SUBLANES, LANES), F32)],
        compiler_params=_cparams(("arbitrary",)), name="mlstm_bwd" if reverse else "mlstm_fwd")(*args)


def _sgu_body(u_ref, v_ref, g_ref, w_ref, b_ref, o_ref):
    u = _gelu_tanh(u_ref[...].astype(F32))
    vg = (_rms(_gelu_tanh(v_ref[...].astype(F32))) * g_ref[...]).astype(BF16)
    for c in range(TOKEN_TILE // CHUNK):
        rows = slice(c * CHUNK, (c + 1) * CHUNK)
        outs = []
        for g in range(SGU_GROUPS):
            cols = slice(g * SGU_GROUP_DIM, (g + 1) * SGU_GROUP_DIM)
            outs.append(_dot(w_ref[g], vg[rows, cols]) + b_ref[:, g:g + 1])
        o_ref[rows, :] = (u[rows, :] * jnp.concatenate(outs, axis=1)).astype(o_ref.dtype)


def _sgu(proj, g_row, w, b_cols):
    n = proj.shape[0]
    wd = BRANCH_WIDTH
    return pl.pallas_call(
        _sgu_body, out_shape=jax.ShapeDtypeStruct((n, wd), BF16), grid=(n // TOKEN_TILE,),
        in_specs=[pl.BlockSpec((TOKEN_TILE, wd), lambda i: (i, PROJ_BLOCKS["sgu_u"])),
                  pl.BlockSpec((TOKEN_TILE, wd), lambda i: (i, PROJ_BLOCKS["sgu_v"])),
                  _resident((1, wd)), _resident(w.shape), _resident(b_cols.shape)],
        out_specs=pl.BlockSpec((TOKEN_TILE, wd), lambda i: (i, 0)),
        compiler_params=_cparams(("parallel",)), name="sgu")(proj, proj, g_row, w, b_cols)


def _merge_body(x_ref, gm_ref, wg_ref, bg_ref, y0_ref, y1_ref, y2_ref, y3_ref, wb_ref, wo_ref, o_ref):
    x = x_ref[...]
    h = (_rms(x) * gm_ref[...]).astype(BF16)
    merged = None
    for i, y_ref in enumerate((y0_ref, y1_ref, y2_ref, y3_ref)):
        cols = slice(i * D_MODEL, (i + 1) * D_MODEL)
        term = _sigmoid(_dot(h, wg_ref[:, cols]) + bg_ref[:, cols]) * _dot(y_ref[...], wb_ref[i])
        merged = term if merged is None else merged + term
    o_ref[...] = x + _dot(merged.astype(BF16), wo_ref[...])


def _merge(x, gm_row, wg, bg_row, branches, wb, wo):
    n, d = x.shape
    br_spec = pl.BlockSpec((TOKEN_TILE, BRANCH_WIDTH), lambda i: (i, 0))
    x_spec = pl.BlockSpec((TOKEN_TILE, d), lambda i: (i, 0))
    return pl.pallas_call(
        _merge_body, out_shape=jax.ShapeDtypeStruct((n, d), F32), grid=(n // TOKEN_TILE,),
        in_specs=[x_spec, _resident((1, d)), _resident(wg.shape), _resident(bg_row.shape),
                  br_spec, br_spec, br_spec, br_spec, _resident(wb.shape), _resident(wo.shape)],
        out_specs=x_spec, compiler_params=_cparams(("parallel",)), name="merge")(
            x, gm_row, wg, bg_row, *branches, wb, wo)


def _ffn_body(x_ref, xp_ref, xn_ref, g_ref, wu_ref, cw_ref, cb_ref, wd_ref, gf_ref, o_ref, act_ref, sa_ref,
              sb_ref, *, geom, final):
    first, last = _tile_ends(geom)
    t = TOKEN_TILE
    hp = SUBLANES
    x = x_ref[...]
    h = (_rms(jnp.concatenate([xp_ref[...], x, xn_ref[...]], axis=0)) * g_ref[...]).astype(BF16)
    pad = FFN_CONV_W // 2

    def project(s_ref, slot, cols):
        up = _dot(h, wu_ref[:, cols])
        s_ref[slot, 0:hp, :] = jnp.where(first, 0.0, up[0:hp, :])
        s_ref[slot, hp:hp + t, :] = up[hp:hp + t, :]
        s_ref[slot, hp + t:, :] = jnp.where(last, 0.0, up[hp + t:, :])

    def project_chunk(c):
        c0 = c * FFN_COL_CHUNK
        project(sa_ref, c % 2, slice(c0, c0 + FFN_COL_CHUNK))
        project(sb_ref, c % 2, slice(D_FF + c0, D_FF + c0 + FFN_COL_CHUNK))

    def conv3(s_ref, slot, cols):
        acc = cb_ref[:, cols]
        for k in range(FFN_CONV_W):
            acc = acc + cw_ref[k:k + 1, cols] * s_ref[slot, hp + k - pad:hp + k - pad + t, :]
        return acc

    o_ref[...] = x
    n_chunks = D_FF // FFN_COL_CHUNK
    per_group = FFN_DOWN_GROUP // FFN_COL_CHUNK
    project_chunk(0)
    for c in range(n_chunks):
        if c + 1 < n_chunks:
            project_chunk(c + 1)
        c0 = c * FFN_COL_CHUNK
        a = conv3(sa_ref, c % 2, slice(c0, c0 + FFN_COL_CHUNK))
        b = conv3(sb_ref, c % 2, slice(D_FF + c0, D_FF + c0 + FFN_COL_CHUNK))
        k0 = (c % per_group) * FFN_COL_CHUNK
        grp = c // per_group
        act_ref[grp % 2, :, k0:k0 + FFN_COL_CHUNK] = (_gelu_tanh(a) * b).astype(BF16)
        if c % per_group == per_group - 1 or c == n_chunks - 1:
            g0 = grp * FFN_DOWN_GROUP
            o_ref[...] += _dot(act_ref[grp % 2, :, :k0 + FFN_COL_CHUNK], wd_ref[g0:c0 + FFN_COL_CHUNK, :])
    if final:
        o_ref[...] = _rms(o_ref[...]) * gf_ref[...]


def _ffn(x, g_row, w_up, conv_w, conv_b, w_down, gf_row, geom, *, final):
    assert conv_w.shape[0] == FFN_CONV_W
    n, d = x.shape
    per = TOKEN_TILE // SUBLANES
    nh = n // SUBLANES
    x_spec = pl.BlockSpec((TOKEN_TILE, d), lambda i: (i, 0))
    halo_rows = TOKEN_TILE + 2 * SUBLANES
    return pl.pallas_call(
        functools.partial(_ffn_body, geom=geom, final=final),
        out_shape=jax.ShapeDtypeStruct((n, d), F32), grid=(n // TOKEN_TILE,),
        in_specs=[x_spec,
                  pl.BlockSpec((SUBLANES, d), lambda i: (jnp.maximum(i * per - 1, 0), 0)),
                  pl.BlockSpec((SUBLANES, d), lambda i: (jnp.minimum((i + 1) * per, nh - 1), 0)),
                  _resident((1, d)), _resident(w_up.shape), _resident(conv_w.shape),
                  _resident((1, 2 * D_FF)), _resident(w_down.shape), _resident((1, d))],
        out_specs=x_spec,
        scratch_shapes=[pltpu.VMEM((2, TOKEN_TILE, FFN_DOWN_GROUP), BF16),
                        pltpu.VMEM((2, halo_rows, FFN_COL_CHUNK), F32),
                        pltpu.VMEM((2, halo_rows, FFN_COL_CHUNK), F32)],
        compiler_params=_cparams(("parallel",)), name="ffn")(
            x, x, x, g_row, w_up, conv_w, conv_b.reshape(1, 2 * D_FF), w_down, gf_row)


def _pad_row(vec, offset):
    return jnp.zeros((1, LANES), F32).at[0, offset:offset + vec.shape[0]].set(vec.astype(F32))


def kernel(x_prompt, x_sample, norm_mix_g, w_in, na_rpb, ssm_conv_w, ssm_conv_b, ssm_dt_bias, ssm_a_log, ssm_d, ssm_norm_g, ml_igate_b, ml_fgate_b, ml_norm_g, sgu_norm_g, sgu_w, sgu_b, w_branch, w_gate, b_gate, w_out, norm_ffn_g, ffn_w_up, ffn_conv_w, ffn_conv_b, ffn_w_down, norm_final_g):
    bp, lp, d = x_prompt.shape
    bs, ls, _ = x_sample.shape
    assert d == D_MODEL and lp % TOKEN_TILE == 0 and ls % TOKEN_TILE == 0
    assert min(lp, ls) >= 2 * TOKEN_TILE
    geom = (bp * lp, lp, ls)
    n = bp * lp + bs * ls
    for st in range(1, N_STREAMS):
        split = st * n // N_STREAMS
        assert n % N_STREAMS == 0 and (split % lp == 0 if split <= bp * lp else (split - bp * lp) % ls == 0)
    depth = w_in.shape[0]
    x = None

    o_ssm, o_ml, o_sgu = NA_COLS, NA_COLS + SSM_COLS, NA_COLS + SSM_COLS + ML_COLS
    bc_lo = o_ssm + 2 * BRANCH_WIDTH
    dt_lo = bc_lo + BC_W
    mlg_lo = o_ml + 4 * BRANCH_WIDTH
    for l in range(depth):
        wl = w_in[l]
        w_main = jnp.concatenate([wl[:, :bc_lo], wl[:, o_ml:mlg_lo], wl[:, o_sgu:], wl[:, bc_lo:dt_lo]],
                                 axis=1).astype(BF16)
        w_small = jnp.concatenate([wl[:, dt_lo:o_ml], wl[:, mlg_lo:o_sgu],
                                   jnp.zeros((d, LANES - 2 * SSM_HEADS - 4 * ML_HEADS), F32)],
                                  axis=1).astype(BF16)
        if l == 0:
            proj, small, x = _in_proj((x_prompt.reshape(bp * lp, d), x_sample.reshape(bs * ls, d)),
                                      norm_mix_g[l], w_main, w_small)
        else:
            proj, small = _in_proj((x,), norm_mix_g[l], w_main, w_small)

        y_na = _neighbourhood_attention(proj, _na_bias_table(na_rpb[l]), geom)

        xc = _conv_silu(proj, PROJ_BLOCKS["ssm_x"], BRANCH_WIDTH, ssm_conv_w[l][:, :BRANCH_WIDTH],
                        ssm_conv_b[l][:BRANCH_WIDTH], geom, name="ssm_conv_x")
        bcc = _conv_silu(proj, BC_BLOCK, BC_W, ssm_conv_w[l][:, BRANCH_WIDTH:], ssm_conv_b[l][BRANCH_WIDTH:],
                         geom, name="ssm_conv_bc")
        dtb_row = _pad_row(ssm_dt_bias[l].reshape(-1), SMALL_DT)
        a_neg = -jnp.exp(ssm_a_log[l].astype(F32))
        proj_s, small_s, xc_s, bcc_s = _streams(proj), _streams(small), _streams(xc), _streams(bcc)
        y_f = _ssd(xc_s, bcc_s, small_s, dtb_row, _pad_row(a_neg[0], SMALL_DT), geom, reverse=False)
        d_row = jnp.repeat(ssm_d[l].astype(F32), SSM_HEAD_DIM).reshape(1, BRANCH_WIDTH)
        y_ssm = _ssd(xc_s, bcc_s, small_s, dtb_row, _pad_row(a_neg[1], SMALL_DT + SSM_HEADS), geom, reverse=True,
                     y_fwd=y_f, proj=proj_s, d_row=d_row, g_row=ssm_norm_g[l].reshape(1, BRANCH_WIDTH))
        y_ssm = y_ssm.reshape(n, BRANCH_WIDTH)

        gate_bias = _pad_row(jnp.concatenate([ml_igate_b[l].reshape(-1), ml_fgate_b[l].reshape(-1)]), SMALL_IG)
        h_f = _mlstm(proj_s, small_s, gate_bias, geom, reverse=False)
        y_ml = _mlstm(proj_s, small_s, gate_bias, geom, reverse=True, h_fwd=h_f,
                      g_row=ml_norm_g[l].reshape(1, BRANCH_WIDTH)).reshape(n, BRANCH_WIDTH)

        sgu_b_cols = jnp.zeros((CHUNK, LANES), F32).at[:, :SGU_GROUPS].set(sgu_b[l].T.astype(F32))
        y_sgu = _sgu(proj, sgu_norm_g[l].reshape(1, BRANCH_WIDTH), sgu_w[l].astype(BF16), sgu_b_cols)

        x = _merge(x, norm_mix_g[l].reshape(1, d), w_gate[l].astype(BF16), b_gate[l].reshape(1, N_BRANCH * d),
                   (y_na, y_ssm, y_ml, y_sgu), w_branch[l].astype(BF16), w_out[l].astype(BF16))

        x = _ffn(x, norm_ffn_g[l].reshape(1, d), ffn_w_up[l].astype(BF16), ffn_conv_w[l], ffn_conv_b[l],
                 ffn_w_down[l].astype(BF16), norm_final_g.reshape(1, d), geom, final=(l == depth - 1))

    return (x[:bp * lp].reshape(bp, lp, d), x[bp * lp:].reshape(bs, ls, d))
```
